```python
import math
import numpy as np
import jax
import jax.numpy as jnp
from jax import lax

D_MODEL = 1024
BATCH = 8
SEQ = 4096
DEPTH = 2

CHUNK = 64
Q_BLOCK = 128
HEAD_DIM = 64
MIX_WIDTH = D_MODEL

SB_HEADS = 6
SB_WIDTH = SB_HEADS * HEAD_DIM
CA_HEADS = 6
CA_WIDTH = CA_HEADS * HEAD_DIM
CA_LEFT_CHUNKS = 8
CA_BAND = (CA_LEFT_CHUNKS + 1) * CHUNK
MAX_REL = 256
DA_HEADS = 4
DA_VDIM = 64
DA_QKDIM = 32
DA_WIDTH = DA_HEADS * DA_VDIM

IN_WIDTH = 4 * (SB_WIDTH + CA_WIDTH + DA_WIDTH)
ROPE_THETA = 10000.0
EPS = 1e-6

kernel_name = "hybrid_stickbreak_chunkrel_diffattn_block"


def rmsnorm(x, g):
    xf = x.astype(jnp.float32)
    y = xf * lax.rsqrt(jnp.mean(xf * xf, axis=-1, keepdims=True) + EPS)
    return (y * g.astype(jnp.float32)).astype(x.dtype)


def rotary(x, positions):
    d = x.shape[-1]
    half = d // 2
    inv_freq = ROPE_THETA ** (-np.arange(half, dtype=np.float32) / half)
    ang = positions.astype(jnp.float32)[..., None] * inv_freq
    ang = ang.reshape(ang.shape[:2] + (1,) * (x.ndim - 3) + (half,))
    cos, sin = jnp.cos(ang), jnp.sin(ang)
    xf = x.astype(jnp.float32)
    x1, x2 = xf[..., :half], xf[..., half:]
    return jnp.concatenate([x1 * cos - x2 * sin, x2 * cos + x1 * sin], axis=-1).astype(x.dtype)


def stick_breaking_attention(q, k, v):
    _, _, s_len, d = q.shape
    scale = d ** -0.5
    outs = []
    for i in range(s_len // Q_BLOCK):
        lo, hi = i * Q_BLOCK, (i + 1) * Q_BLOCK
        z = jnp.einsum('bhqd,bhkd->bhqk', q[:, :, lo:hi], k[:, :, :hi]).astype(jnp.float32) * scale
        strict = np.arange(hi)[None, :] < np.arange(lo, hi)[:, None]
        log_1m_beta = jnp.where(strict, jax.nn.log_sigmoid(-z), 0.0)
        between = lax.cumsum(log_1m_beta, axis=3, reverse=True) - log_1m_beta
        w = jnp.where(strict, jnp.exp(jax.nn.log_sigmoid(z) + between), 0.0)
        outs.append(jnp.einsum('bhqk,bhkd->bhqd', w.astype(v.dtype), v[:, :, :hi]))
    return jnp.concatenate(outs, axis=2)


def chunk_relpos_attention(q, k, v, rel_table):
    b, h, s_len, d = q.shape
    nc = s_len // CHUNK
    scale = d ** -0.5
    qc = q.reshape(b, h, nc, CHUNK, d)
    pad = ((0, 0), (0, 0), (CA_LEFT_CHUNKS * CHUNK, 0), (0, 0))
    kc = jnp.pad(k, pad).reshape(b, h, nc + CA_LEFT_CHUNKS, CHUNK, d)
    vc = jnp.pad(v, pad).reshape(b, h, nc + CA_LEFT_CHUNKS, CHUNK, d)
    k_band = jnp.concatenate([kc[:, :, o:o + nc] for o in range(CA_LEFT_CHUNKS + 1)], axis=3)
    v_band = jnp.concatenate([vc[:, :, o:o + nc] for o in range(CA_LEFT_CHUNKS + 1)], axis=3)
    scores = jnp.einsum('bhcqd,bhckd->bhcqk', qc, k_band).astype(jnp.float32) * scale
    rel = (CA_LEFT_CHUNKS * CHUNK + np.arange(CHUNK)[:, None]) - np.arange(CA_BAND)[None, :]
    rel_idx = np.clip(rel, -MAX_REL, MAX_REL) + MAX_REL
    bias = rel_table.astype(jnp.float32)[:, rel_idx]
    key_chunk = np.arange(nc)[:, None] - CA_LEFT_CHUNKS + (np.arange(CA_BAND) // CHUNK)[None, :]
    valid = key_chunk >= 0
    scores = jnp.where(valid[None, None, :, None, :], scores + bias[None, :, None], -jnp.inf)
    probs = jax.nn.softmax(scores, axis=-1).astype(v.dtype)
    out = jnp.einsum('bhcqk,bhckd->bhcqd', probs, v_band)
    return out.reshape(b, h, s_len, d)


def diff_attention(q, k, v, lam, lam_init, subln_g):
    _, _, _, s_len, dk = q.shape
    scale = dk ** -0.5
    outs = []
    for i in range(s_len // Q_BLOCK):
        lo, hi = i * Q_BLOCK, (i + 1) * Q_BLOCK
        s = jnp.einsum('bhnqd,bhnkd->bhnqk', q[:, :, :, lo:hi], k[:, :, :, :hi]).astype(jnp.float32) * scale
        mask = (np.arange(hi) // CHUNK)[None, :] <= (np.arange(lo, hi) // CHUNK)[:, None]
        p = jax.nn.softmax(jnp.where(mask, s, -jnp.inf), axis=-1)
        w = p[:, :, 0] - lam * p[:, :, 1]
        outs.append(jnp.einsum('bhqk,bhkd->bhqd', w.astype(v.dtype), v[:, :, :hi]))
    o = jnp.concatenate(outs, axis=2)
    return rmsnorm(o, subln_g) * (1.0 - lam_init)


def hybrid_layer(x, positions, w_in, w_out, pre_g, post_g, rel_table,
                 lq1, lk1, lq2, lk2, subln_g, lam_init):
    b, s_len, _ = x.shape
    h = rmsnorm(x, pre_g)
    proj = h @ w_in
    a_part = proj[..., :4 * SB_WIDTH]
    b_part = proj[..., 4 * SB_WIDTH:4 * (SB_WIDTH + CA_WIDTH)]
    c_part = proj[..., 4 * (SB_WIDTH + CA_WIDTH):]

    def heads(t, n):
        return t.reshape(b, s_len, n, -1).transpose(0, 2, 1, 3)

    def merge(t):
        return t.transpose(0, 2, 1, 3).reshape(b, s_len, -1)

    qa, ka, va, ga = jnp.split(a_part, 4, axis=-1)
    ya = stick_breaking_attention(heads(qa, SB_HEADS), heads(ka, SB_HEADS), heads(va, SB_HEADS))
    ya = merge(ya) * jax.nn.silu(ga)

    qb, kb, vb, gb = jnp.split(b_part, 4, axis=-1)
    yb = chunk_relpos_attention(heads(qb, CA_HEADS), heads(kb, CA_HEADS), heads(vb, CA_HEADS), rel_table)
    yb = merge(yb) * jax.nn.silu(gb)

    qc, kc, vc, gc = jnp.split(c_part, 4, axis=-1)
    qc = rotary(qc.reshape(b, s_len, DA_HEADS, 2, DA_QKDIM), positions).transpose(0, 2, 3, 1, 4)
    kc = rotary(kc.reshape(b, s_len, DA_HEADS, 2, DA_QKDIM), positions).transpose(0, 2, 3, 1, 4)
    lam = (jnp.exp(jnp.sum(lq1.astype(jnp.float32) * lk1.astype(jnp.float32)))
           - jnp.exp(jnp.sum(lq2.astype(jnp.float32) * lk2.astype(jnp.float32))) + lam_init)
    yc = diff_attention(qc, kc, heads(vc, DA_HEADS), lam, lam_init, subln_g)
    yc = merge(yc) * jax.nn.silu(gc)

    y = jnp.concatenate([ya, yb, yc], axis=-1) @ w_out
    return x + rmsnorm(y, post_g)


def setup_inputs(seed: int = 0) -> dict:
    key = jax.random.key(seed)
    ks = jax.random.split(key, 12)
    x = jax.random.normal(ks[0], (BATCH, SEQ, D_MODEL), jnp.float32)
    positions = jnp.broadcast_to(jnp.arange(SEQ, dtype=jnp.int32), (BATCH, SEQ))
    w_in = jax.random.normal(ks[1], (DEPTH, D_MODEL, IN_WIDTH), jnp.float32) * D_MODEL ** -0.5
    w_out = jax.random.normal(ks[2], (DEPTH, MIX_WIDTH, D_MODEL), jnp.float32) * MIX_WIDTH ** -0.5
    pre_gain = 1.0 + 0.02 * jax.random.normal(ks[3], (DEPTH, D_MODEL), jnp.float32)
    post_gain = 1.0 + 0.02 * jax.random.normal(ks[4], (DEPTH, D_MODEL), jnp.float32)
    rel_bias = 0.1 * jax.random.normal(ks[5], (DEPTH, CA_HEADS, 2 * MAX_REL + 1), jnp.float32)
    lambda_q1 = 0.1 * jax.random.normal(ks[6], (DEPTH, DA_QKDIM), jnp.float32)
    lambda_k1 = 0.1 * jax.random.normal(ks[7], (DEPTH, DA_QKDIM), jnp.float32)
    lambda_q2 = 0.1 * jax.random.normal(ks[8], (DEPTH, DA_QKDIM), jnp.float32)
    lambda_k2 = 0.1 * jax.random.normal(ks[9], (DEPTH, DA_QKDIM), jnp.float32)
    subln_gain = 1.0 + 0.02 * jax.random.normal(ks[10], (DEPTH, DA_VDIM), jnp.float32)
    return {"x": x, "positions": positions, "w_in": w_in, "w_out": w_out,
            "pre_gain": pre_gain, "post_gain": post_gain, "rel_bias": rel_bias,
            "lambda_q1": lambda_q1, "lambda_k1": lambda_k1, "lambda_q2": lambda_q2,
            "lambda_k2": lambda_k2, "subln_gain": subln_gain}


def reference(x, positions, w_in, w_out, pre_gain, post_gain, rel_bias,
              lambda_q1, lambda_k1, lambda_q2, lambda_k2, subln_gain):
    for layer in range(DEPTH):
        lam_init = 0.8 - 0.6 * math.exp(-0.3 * layer)
        x = hybrid_layer(x, positions, w_in[layer], w_out[layer], pre_gain[layer], post_gain[layer],
                         rel_bias[layer], lambda_q1[layer], lambda_k1[layer], lambda_q2[layer],
                         lambda_k2[layer], subln_gain[layer], lam_init)
    return x
```

```python
import functools
import math

import numpy as np
import jax
import jax.numpy as jnp
from jax import lax
from jax.experimental import pallas as pl
from jax.experimental.pallas import tpu as pltpu

D_MODEL = 1024
CHUNK = 64
HEAD_DIM = 64
SB_HEADS = 6
CA_HEADS = 6
CA_LEFT_CHUNKS = 8
MAX_REL = 256
DA_HEADS = 4
DA_QKDIM = 32
ROPE_THETA = 10000.0
EPS = 1e-6

SB_WIDTH = SB_HEADS * HEAD_DIM
CA_WIDTH = CA_HEADS * HEAD_DIM
DA_WIDTH = DA_HEADS * HEAD_DIM
QVG_ROWS = 3 * (SB_WIDTH + CA_WIDTH + DA_WIDTH)
K_COLS = SB_WIDTH + CA_WIDTH + DA_WIDTH

ATTN_TILE = 256
PROJ_TILE = 512
LANES = 128
VMEM_LIMIT = 48 * 1024 * 1024

ROW_QA, ROW_VA, ROW_GA = 0, SB_WIDTH, 2 * SB_WIDTH
ROW_QB = 3 * SB_WIDTH
ROW_VB, ROW_GB = ROW_QB + CA_WIDTH, ROW_QB + 2 * CA_WIDTH
ROW_QC = ROW_QB + 3 * CA_WIDTH
ROW_VC, ROW_GC = ROW_QC + DA_WIDTH, ROW_QC + 2 * DA_WIDTH
COL_KA, COL_KB, COL_KC = 0, SB_WIDTH, SB_WIDTH + CA_WIDTH

F32 = jnp.float32
BF16 = jnp.bfloat16
NEG_INF = float("-inf")


def _nt_dot(a, b):
    return lax.dot_general(a, b, (((1,), (1,)), ((), ())), preferred_element_type=F32)


def _dot(a, b):
    return jnp.dot(a, b, preferred_element_type=F32)


def _silu(g):
    return g * (1.0 / (1.0 + jnp.exp(-g)))


def _select_rows(q_pair, lo, hi):
    row = lax.broadcasted_iota(jnp.int32, q_pair.shape, 0)
    return jnp.where((row >= lo) & (row < hi), q_pair, jnp.zeros_like(q_pair))


def _in_proj_kernel(x_ref, g_ref, wt_ref, wk_ref, wkct_ref, pos_ref, invf_ref, qvg_ref, k_ref):
    x = x_ref[0]
    ms = jnp.mean(x * x, axis=-1, keepdims=True)
    h = (x * lax.rsqrt(ms + EPS) * g_ref[...]).astype(BF16)
    ts = h.shape[0]

    ang = pos_ref[0].astype(F32) * invf_ref[...]
    cos, sin = jnp.cos(ang), jnp.sin(ang)
    half = DA_QKDIM // 2

    def rotary_t(t):
        parts = []
        for grp in range(DA_WIDTH // DA_QKDIM):
            x1 = t[grp * DA_QKDIM:grp * DA_QKDIM + half]
            x2 = t[grp * DA_QKDIM + half:(grp + 1) * DA_QKDIM]
            parts.append(x1 * cos - x2 * sin)
            parts.append(x2 * cos + x1 * sin)
        return jnp.concatenate(parts, axis=0)

    def store_rows(r0, val):
        for t in range(ts // ATTN_TILE):
            qvg_ref[0, t, r0:r0 + val.shape[0], :] = val[:, t * ATTN_TILE:(t + 1) * ATTN_TILE].astype(BF16)

    chunk = SB_WIDTH
    for r0 in range(0, QVG_ROWS, chunk):
        res = _nt_dot(wt_ref[r0:r0 + chunk, :], h)
        if r0 in (ROW_QA, ROW_QB):
            res = res * (HEAD_DIM ** -0.5)
            store_rows(r0, res)
        elif r0 == ROW_QC:
            store_rows(r0, rotary_t(res[:DA_WIDTH]))
            store_rows(r0 + DA_WIDTH, res[DA_WIDTH:])
        else:
            store_rows(r0, res)

    kab = _dot(h, wk_ref[...])
    k_ref[0, :, 0:COL_KC] = kab.astype(BF16)
    kct = rotary_t(_nt_dot(wkct_ref[...], h))
    k_ref[0, :, COL_KC:K_COLS] = kct.T.astype(BF16)


def _in_proj(x, pre_g, wt, wk, wkct, pos3, invf):
    b, s, d = x.shape
    ts = min(PROJ_TILE, s)
    ns = s // ATTN_TILE
    const = lambda *_: (0, 0)
    return pl.pallas_call(
        _in_proj_kernel,
        out_shape=(jax.ShapeDtypeStruct((b, ns, QVG_ROWS, ATTN_TILE), BF16),
                   jax.ShapeDtypeStruct((b, s, K_COLS), BF16)),
        grid=(b, s // ts),
        in_specs=[
            pl.BlockSpec((1, ts, d), lambda bi, si: (bi, si, 0)),
            pl.BlockSpec((1, d), const),
            pl.BlockSpec(wt.shape, const),
            pl.BlockSpec(wk.shape, const),
            pl.BlockSpec(wkct.shape, const),
            pl.BlockSpec((1, 1, ts), lambda bi, si: (bi, 0, si)),
            pl.BlockSpec(invf.shape, const),
        ],
        out_specs=(
            pl.BlockSpec((1, ts // ATTN_TILE, QVG_ROWS, ATTN_TILE), lambda bi, si: (bi, si, 0, 0)),
            pl.BlockSpec((1, ts, K_COLS), lambda bi, si: (bi, si, 0)),
        ),
        compiler_params=pltpu.CompilerParams(
            dimension_semantics=("parallel", "parallel"), vmem_limit_bytes=VMEM_LIMIT),
        name="in_proj",
    )(x, pre_g, wt, wk, wkct, pos3, invf)


def _sb_kernel(q_ref, k_ref, v_ref, g_ref, tri_ref, o_ref):
    i = pl.program_id(2)
    par = pl.program_id(1) % 2
    t = ATTN_TILE
    q = _select_rows(q_ref[0, 0], par * HEAD_DIM, (par + 1) * HEAD_DIM)
    tri = tri_ref[...]

    def tile(j, carry, acc, diag):
        kt = k_ref[0, pl.ds(pl.multiple_of(j * t, t), t), :]
        s = _dot(kt, q)
        sp = jnp.maximum(s, 0.0) + jnp.log(1.0 + jnp.exp(-jnp.abs(s)))
        l = -sp
        if diag:
            strict = (lax.broadcasted_iota(jnp.int32, (t, t), 0)
                      < lax.broadcasted_iota(jnp.int32, (t, t), 1))
            l = jnp.where(strict, l, 0.0)
        l_hi = l.astype(BF16)
        l_lo = (l - l_hi.astype(F32)).astype(BF16)
        between = _dot(tri, l_hi) + _dot(tri, l_lo)
        w = jnp.exp((s - sp) + between + carry)
        if diag:
            w = jnp.where(strict, w, 0.0)
        acc = acc + _dot(v_ref[0, j], w.astype(BF16))
        carry = carry + jnp.sum(l, axis=0, keepdims=True)
        return carry, acc

    carry, acc = tile(i, jnp.zeros((1, t), F32), jnp.zeros((HEAD_DIM, t), F32), True)

    def body(n, state):
        return tile(i - 1 - n, state[0], state[1], False)

    carry, acc = lax.fori_loop(0, i, body, (carry, acc))
    o_ref[0, 0] = (acc * _silu(g_ref[0, 0].astype(F32))).astype(BF16)


def _sb_attention(qvg, k, tri):
    b, ns, _, t = qvg.shape
    s = k.shape[1]
    return pl.pallas_call(
        _sb_kernel,
        out_shape=jax.ShapeDtypeStruct((b, ns, SB_WIDTH, t), BF16),
        grid=(b, SB_HEADS, ns),
        in_specs=[
            pl.BlockSpec((1, 1, LANES, t), lambda bi, h, i: (bi, i, ROW_QA // LANES + h // 2, 0)),
            pl.BlockSpec((1, s, LANES), lambda bi, h, i: (bi, 0, COL_KA // LANES + h // 2)),
            pl.BlockSpec((1, ns, HEAD_DIM, t), lambda bi, h, i: (bi, 0, ROW_VA // HEAD_DIM + h, 0)),
            pl.BlockSpec((1, 1, HEAD_DIM, t), lambda bi, h, i: (bi, i, ROW_GA // HEAD_DIM + h, 0)),
            pl.BlockSpec((t, t), lambda bi, h, i: (0, 0)),
        ],
        out_specs=pl.BlockSpec((1, 1, HEAD_DIM, t), lambda bi, h, i: (bi, i, h, 0)),
        compiler_params=pltpu.CompilerParams(
            dimension_semantics=("parallel", "parallel", "arbitrary"), vmem_limit_bytes=VMEM_LIMIT),
        name="sb_attn",
    )(qvg, k, qvg, qvg, tri)


CA_BLOCKS = CA_LEFT_CHUNKS * CHUNK // ATTN_TILE + 1


def _ca_kernel(q_ref, k0_ref, k1_ref, k2_ref, v0_ref, v1_ref, v2_ref, g_ref, bias_ref, o_ref):
    i = pl.program_id(2)
    par = pl.program_id(1) % 2
    q = _select_rows(q_ref[0, 0], par * HEAD_DIM, (par + 1) * HEAD_DIM)
    k_refs = (k0_ref, k1_ref, k2_ref)
    v_refs = (v0_ref, v1_ref, v2_ref)

    scores = []
    for d in range(CA_BLOCKS):
        s = _dot(k_refs[d][0], q) + bias_ref[0, d]
        if d < CA_BLOCKS - 1:
            s = jnp.where(i - (CA_BLOCKS - 1) + d >= 0, s, NEG_INF)
        scores.append(s)
    m = functools.reduce(jnp.maximum, [jnp.max(s, axis=0, keepdims=True) for s in scores])
    denom = jnp.zeros_like(m)
    acc = jnp.zeros((HEAD_DIM, q.shape[1]), F32)
    for d in range(CA_BLOCKS):
        p = jnp.exp(scores[d] - m)
        denom = denom + jnp.sum(p, axis=0, keepdims=True)
        acc = acc + _dot(v_refs[d][0, 0], p.astype(BF16))
    o_ref[0, 0] = ((acc / denom) * _silu(g_ref[0, 0].astype(F32))).astype(BF16)


def _ca_attention(qvg, k, bias):
    b, ns, _, t = qvg.shape
    back = CA_BLOCKS - 1

    def kspec(d):
        return pl.BlockSpec((1, t, LANES),
                            lambda bi, h, i: (bi, jnp.maximum(i - back + d, 0), COL_KB // LANES + h // 2))

    def vspec(d):
        return pl.BlockSpec((1, 1, HEAD_DIM, t),
                            lambda bi, h, i: (bi, jnp.maximum(i - back + d, 0), ROW_VB // HEAD_DIM + h, 0))

    return pl.pallas_call(
        _ca_kernel,
        out_shape=jax.ShapeDtypeStruct((b, ns, CA_WIDTH, t), BF16),
        grid=(b, CA_HEADS, ns),
        in_specs=[pl.BlockSpec((1, 1, LANES, t), lambda bi, h, i: (bi, i, ROW_QB // LANES + h // 2, 0))]
        + [kspec(d) for d in range(CA_BLOCKS)]
        + [vspec(d) for d in range(CA_BLOCKS)]
        + [pl.BlockSpec((1, 1, HEAD_DIM, t), lambda bi, h, i: (bi, i, ROW_GB // HEAD_DIM + h, 0)),
           pl.BlockSpec((1, CA_BLOCKS, t, t), lambda bi, h, i: (h, 0, 0, 0))],
        out_specs=pl.BlockSpec((1, 1, HEAD_DIM, t), lambda bi, h, i: (bi, i, h, 0)),
        compiler_params=pltpu.CompilerParams(
            dimension_semantics=("parallel", "parallel", "arbitrary"), vmem_limit_bytes=VMEM_LIMIT),
        name="ca_attn",
    )(qvg, k, k, k, qvg, qvg, qvg, qvg, bias)


def _ca_bias(rel_table):
    t = ATTN_TILE
    jk = np.arange(t)[:, None]
    iq = np.arange(t)[None, :]
    tiles = []
    for d in range(CA_BLOCKS):
        rel = (CA_BLOCKS - 1 - d) * t + iq - jk
        idx = np.clip(rel, -MAX_REL, MAX_REL) + MAX_REL
        lag = (CA_BLOCKS - 1 - d) * (t // CHUNK) + iq // CHUNK - jk // CHUNK
        valid = (lag >= 0) & (lag <= CA_LEFT_CHUNKS)
        tiles.append(jnp.where(valid[None], rel_table.astype(F32)[:, idx], NEG_INF))
    return jnp.stack(tiles, axis=1)


def _da_kernel(q_ref, k_ref, v_ref, g_ref, lamv_ref, subg_ref, o_ref, *, lam_init):
    i = pl.program_id(2)
    par = pl.program_id(1) % 2
    t = ATTN_TILE
    scale = DA_QKDIM ** -0.5
    q_pair = q_ref[0, 0]
    base = par * HEAD_DIM
    qs = (_select_rows(q_pair, base, base + DA_QKDIM),
          _select_rows(q_pair, base + DA_QKDIM, base + 2 * DA_QKDIM))

    def tile(j, state, diag):
        kt = k_ref[0, pl.ds(pl.multiple_of(j * t, t), t), :]
        vt = v_ref[0, j]
        if diag:
            mask = (lax.broadcasted_iota(jnp.int32, (t, t), 0) // CHUNK
                    <= lax.broadcasted_iota(jnp.int32, (t, t), 1) // CHUNK)
        new = []
        for n in range(2):
            m, l, acc = state[n]
            s = _dot(kt, qs[n]) * scale
            if diag:
                s = jnp.where(mask, s, NEG_INF)
            m_new = jnp.maximum(m, jnp.max(s, axis=0, keepdims=True))
            alpha = jnp.exp(m - m_new)
            p = jnp.exp(s - m_new)
            l = alpha * l + jnp.sum(p, axis=0, keepdims=True)
            acc = alpha * acc + _dot(vt, p.astype(BF16))
            new.append((m_new, l, acc))
        return tuple(new)

    init = tuple((jnp.full((1, t), NEG_INF, F32), jnp.zeros((1, t), F32), jnp.zeros((HEAD_DIM, t), F32))
                 for _ in range(2))
    state = tile(i, init, True)
    state = lax.fori_loop(0, i, lambda n, st: tile(n, st, False), state)

    lamv = lamv_ref[...]
    lam = (jnp.exp(jnp.sum(lamv[0:1] * lamv[1:2], axis=-1, keepdims=True))
           - jnp.exp(jnp.sum(lamv[2:3] * lamv[3:4], axis=-1, keepdims=True)) + lam_init)
    (_, l1, a1), (_, l2, a2) = state
    o = a1 / l1 - lam * (a2 / l2)
    o = o * lax.rsqrt(jnp.mean(o * o, axis=0, keepdims=True) + EPS) * subg_ref[...]
    o = o * (1.0 - lam_init)
    o_ref[0, 0] = (o * _silu(g_ref[0, 0].astype(F32))).astype(BF16)


def _da_attention(qvg, k, lamv, subg, lam_init):
    b, ns, _, t = qvg.shape
    s = k.shape[1]
    return pl.pallas_call(
        functools.partial(_da_kernel, lam_init=lam_init),
        out_shape=jax.ShapeDtypeStruct((b, ns, DA_WIDTH, t), BF16),
        grid=(b, DA_HEADS, ns),
        in_specs=[
            pl.BlockSpec((1, 1, LANES, t), lambda bi, h, i: (bi, i, ROW_QC // LANES + h // 2, 0)),
            pl.BlockSpec((1, s, LANES), lambda bi, h, i: (bi, 0, COL_KC // LANES + h // 2)),
            pl.BlockSpec((1, ns, HEAD_DIM, t), lambda bi, h, i: (bi, 0, ROW_VC // HEAD_DIM + h, 0)),
            pl.BlockSpec((1, 1, HEAD_DIM, t), lambda bi, h, i: (bi, i, ROW_GC // HEAD_DIM + h, 0)),
            pl.BlockSpec(lamv.shape, lambda bi, h, i: (0, 0)),
            pl.BlockSpec(subg.shape, lambda bi, h, i: (0, 0)),
        ],
        out_specs=pl.BlockSpec((1, 1, HEAD_DIM, t), lambda bi, h, i: (bi, i, h, 0)),
        compiler_params=pltpu.CompilerParams(
            dimension_semantics=("parallel", "parallel", "arbitrary"), vmem_limit_bytes=VMEM_LIMIT),
        name="da_attn",
    )(qvg, k, qvg, qvg, lamv, subg)


def _out_proj_kernel(ya_ref, yb_ref, yc_ref, wo_ref, g_ref, x_ref, o_ref):
    t = ATTN_TILE
    for n in range(ya_ref.shape[1]):
        y_in = jnp.concatenate([ya_ref[0, n], yb_ref[0, n], yc_ref[0, n]], axis=0)
        yt = _dot(wo_ref[...], y_in)
        yt = yt * lax.rsqrt(jnp.mean(yt * yt, axis=0, keepdims=True) + EPS)
        rows = slice(n * t, (n + 1) * t)
        o_ref[0, rows, :] = x_ref[0, rows, :] + yt.T * g_ref[...]


def _out_proj(ya, yb, yc, wot, post_g, x):
    b, s, d = x.shape
    ts = min(PROJ_TILE, s)
    nt = ts // ATTN_TILE
    const = lambda *_: (0, 0)
    yspec = lambda w: pl.BlockSpec((1, nt, w, ATTN_TILE), lambda bi, si: (bi, si, 0, 0))
    return pl.pallas_call(
        _out_proj_kernel,
        out_shape=jax.ShapeDtypeStruct((b, s, d), F32),
        grid=(b, s // ts),
        in_specs=[yspec(SB_WIDTH), yspec(CA_WIDTH), yspec(DA_WIDTH),
                  pl.BlockSpec(wot.shape, const),
                  pl.BlockSpec((1, d), const),
                  pl.BlockSpec((1, ts, d), lambda bi, si: (bi, si, 0))],
        out_specs=pl.BlockSpec((1, ts, d), lambda bi, si: (bi, si, 0)),
        compiler_params=pltpu.CompilerParams(
            dimension_semantics=("parallel", "parallel"), vmem_limit_bytes=VMEM_LIMIT),
        name="out_proj",
    )(ya, yb, yc, wot, post_g, x)


def _split_w_in(w_in):
    a, bb, c = SB_WIDTH, CA_WIDTH, DA_WIDTH
    o_b, o_c = 4 * a, 4 * (a + bb)
    col = lambda o, w, n: w_in[:, o + n * w:o + (n + 1) * w]
    qvg = jnp.concatenate([col(0, a, 0), col(0, a, 2), col(0, a, 3),
                           col(o_b, bb, 0), col(o_b, bb, 2), col(o_b, bb, 3),
                           col(o_c, c, 0), col(o_c, c, 2), col(o_c, c, 3)], axis=1)
    wt = qvg.T.astype(BF16)
    wk = jnp.concatenate([col(0, a, 1), col(o_b, bb, 1)], axis=1).astype(BF16)
    wkct = col(o_c, c, 1).T.astype(BF16)
    return wt, wk, wkct


def _layer(x, pos3, invf, tri, w_in, w_out, pre_g, post_g, rel_table, lamv, subln_g, lam_init):
    wt, wk, wkct = _split_w_in(w_in)
    qvg, k = _in_proj(x, pre_g[None, :], wt, wk, wkct, pos3, invf)
    ya = _sb_attention(qvg, k, tri)
    yb = _ca_attention(qvg, k, _ca_bias(rel_table))
    yc = _da_attention(qvg, k, lamv, subln_g[:, None], lam_init)
    return _out_proj(ya, yb, yc, w_out.T.astype(BF16), post_g[None, :], x)


def kernel(x, positions, w_in, w_out, pre_gain, post_gain, rel_bias, lambda_q1, lambda_k1, lambda_q2,
           lambda_k2, subln_gain):
    b, s, d = x.shape
    assert d == D_MODEL and s % PROJ_TILE == 0 or s % ATTN_TILE == 0
    half = DA_QKDIM // 2
    invf = jnp.asarray((ROPE_THETA ** (-np.arange(half, dtype=np.float32) / half))[:, None])
    pos3 = positions.reshape(b, 1, s)
    t = ATTN_TILE
    tri = jnp.asarray(np.arange(t)[None, :] > np.arange(t)[:, None], dtype=BF16)
    for layer in range(w_in.shape[0]):
        lam_init = 0.8 - 0.6 * math.exp(-0.3 * layer)
        lamv = jnp.stack([lambda_q1[layer], lambda_k1[layer], lambda_q2[layer], lambda_k2[layer]])
        x = _layer(x, pos3, invf, tri, w_in[layer], w_out[layer], pre_gain[layer], post_gain[layer],
                   rel_bias[layer], lamv, subln_gain[layer], lam_init)
    return x
```

```python
import functools
import math

import numpy as np
import jax
import jax.numpy as jnp
from jax import lax
from jax.experimental import pallas as pl
from jax.experimental.pallas import tpu as pltpu

D_MODEL = 1024
CHUNK = 64
HEAD_DIM = 64
SB_HEADS = 6
CA_HEADS = 6
CA_LEFT_CHUNKS = 8
MAX_REL = 256
DA_HEADS = 4
DA_QKDIM = 32
ROPE_THETA = 10000.0
EPS = 1e-6

SB_WIDTH = SB_HEADS * HEAD_DIM
CA_WIDTH = CA_HEADS * HEAD_DIM
DA_WIDTH = DA_HEADS * HEAD_DIM
QVG_ROWS = 3 * (SB_WIDTH + CA_WIDTH + DA_WIDTH)
K_COLS = SB_WIDTH + CA_WIDTH + DA_WIDTH

ATTN_TILE = 256
PROJ_TILE = 512
LANES = 128
PAIR = LANES // HEAD_DIM
VMEM_LIMIT = 48 * 1024 * 1024

ROW_QA, ROW_VA, ROW_GA = 0, SB_WIDTH, 2 * SB_WIDTH
ROW_QB = 3 * SB_WIDTH
ROW_VB, ROW_GB = ROW_QB + CA_WIDTH, ROW_QB + 2 * CA_WIDTH
ROW_QC = ROW_QB + 3 * CA_WIDTH
ROW_VC, ROW_GC = ROW_QC + DA_WIDTH, ROW_QC + 2 * DA_WIDTH
COL_KA, COL_KB, COL_KC = 0, SB_WIDTH, SB_WIDTH + CA_WIDTH

F32 = jnp.float32
BF16 = jnp.bfloat16
NEG_INF = float("-inf")


def _nt_dot(a, b):
    return lax.dot_general(a, b, (((1,), (1,)), ((), ())), preferred_element_type=F32)


def _dot(a, b):
    return jnp.dot(a, b, preferred_element_type=F32)


def _silu(g):
    return g * (1.0 / (1.0 + jnp.exp(-g)))


def _select_rows(q_pair, lo, hi):
    row = lax.broadcasted_iota(jnp.int32, q_pair.shape, 0)
    return jnp.where((row >= lo) & (row < hi), q_pair, jnp.zeros_like(q_pair))


def _head_rows(hd):
    return slice(hd * HEAD_DIM, (hd + 1) * HEAD_DIM)


def _key_tile(k_ref, j):
    return k_ref[0, pl.ds(pl.multiple_of(j * ATTN_TILE, ATTN_TILE), ATTN_TILE), :]


def _paired_key_loop(i, state, tiles):
    state = tiles(state, [i], True)
    odd = i % 2
    state = lax.cond(odd == 1, lambda st: tiles(st, [i - 1], False), lambda st: st, state)
    top = i - 1 - odd

    def body(n, st):
        j = top - 2 * n
        return tiles(st, [j, j - 1], False)

    return lax.fori_loop(0, i // 2, body, state)


def _attn_specs(ns, s, t, row_q, col_k, row_v, row_g, resident):
    q = pl.BlockSpec((1, 1, LANES, t), lambda bi, p, i: (bi, i, row_q // LANES + p, 0))
    g = pl.BlockSpec((1, 1, LANES, t), lambda bi, p, i: (bi, i, row_g // LANES + p, 0))
    if resident:
        k = pl.BlockSpec((1, s, LANES), lambda bi, p, i: (bi, 0, col_k // LANES + p))
        v = pl.BlockSpec((1, ns, LANES, t), lambda bi, p, i: (bi, 0, row_v // LANES + p, 0))
        return q, k, v, g
    return q, g


def _in_proj_kernel(x_ref, g_ref, wt_ref, wk_ref, wkct_ref, pos_ref, invf_ref, qvg_ref, k_ref):
    x = x_ref[0]
    ms = jnp.mean(x * x, axis=-1, keepdims=True)
    h = (x * lax.rsqrt(ms + EPS) * g_ref[...]).astype(BF16)
    ts = h.shape[0]

    ang = pos_ref[0].astype(F32) * invf_ref[...]
    cos, sin = jnp.cos(ang), jnp.sin(ang)
    half = DA_QKDIM // 2

    def rotary_t(t):
        parts = []
        for grp in range(DA_WIDTH // DA_QKDIM):
            x1 = t[grp * DA_QKDIM:grp * DA_QKDIM + half]
            x2 = t[grp * DA_QKDIM + half:(grp + 1) * DA_QKDIM]
            parts.append(x1 * cos - x2 * sin)
            parts.append(x2 * cos + x1 * sin)
        return jnp.concatenate(parts, axis=0)

    def store_rows(r0, val):
        for t in range(ts // ATTN_TILE):
            qvg_ref[0, t, r0:r0 + val.shape[0], :] = val[:, t * ATTN_TILE:(t + 1) * ATTN_TILE].astype(BF16)

    chunk = SB_WIDTH
    for r0 in range(0, QVG_ROWS, chunk):
        res = _nt_dot(wt_ref[r0:r0 + chunk, :], h)
        if r0 in (ROW_QA, ROW_QB):
            res = res * (HEAD_DIM ** -0.5)
            store_rows(r0, res)
        elif r0 == ROW_QC:
            store_rows(r0, rotary_t(res[:DA_WIDTH]))
            store_rows(r0 + DA_WIDTH, res[DA_WIDTH:])
        else:
            store_rows(r0, res)

    kab = _dot(h, wk_ref[...])
    k_ref[0, :, 0:COL_KC] = kab.astype(BF16)
    kct = rotary_t(_nt_dot(wkct_ref[...], h))
    k_ref[0, :, COL_KC:K_COLS] = kct.T.astype(BF16)


def _in_proj(x, pre_g, wt, wk, wkct, pos3, invf):
    b, s, d = x.shape
    ts = min(PROJ_TILE, s)
    ns = s // ATTN_TILE
    const = lambda *_: (0, 0)
    return pl.pallas_call(
        _in_proj_kernel,
        out_shape=(jax.ShapeDtypeStruct((b, ns, QVG_ROWS, ATTN_TILE), BF16),
                   jax.ShapeDtypeStruct((b, s, K_COLS), BF16)),
        grid=(b, s // ts),
        in_specs=[
            pl.BlockSpec((1, ts, d), lambda bi, si: (bi, si, 0)),
            pl.BlockSpec((1, d), const),
            pl.BlockSpec(wt.shape, const),
            pl.BlockSpec(wk.shape, const),
            pl.BlockSpec(wkct.shape, const),
            pl.BlockSpec((1, 1, ts), lambda bi, si: (bi, 0, si)),
            pl.BlockSpec(invf.shape, const),
        ],
        out_specs=(
            pl.BlockSpec((1, ts // ATTN_TILE, QVG_ROWS, ATTN_TILE), lambda bi, si: (bi, si, 0, 0)),
            pl.BlockSpec((1, ts, K_COLS), lambda bi, si: (bi, si, 0)),
        ),
        compiler_params=pltpu.CompilerParams(
            dimension_semantics=("parallel", "parallel"), vmem_limit_bytes=VMEM_LIMIT),
        name="in_proj",
    )(x, pre_g, wt, wk, wkct, pos3, invf)


def _sb_kernel(q_ref, k_ref, v_ref, g_ref, tri_ref, o_ref):
    i = pl.program_id(2)
    t = ATTN_TILE
    q_pair = q_ref[0, 0]
    qs = [_select_rows(q_pair, hd * HEAD_DIM, (hd + 1) * HEAD_DIM) for hd in range(PAIR)]
    neg_tri = tri_ref[...]

    def tiles(state, js, diag):
        kts = [_key_tile(k_ref, j) for j in js]
        if diag:
            strict = (lax.broadcasted_iota(jnp.int32, (t, t), 0)
                      < lax.broadcasted_iota(jnp.int32, (t, t), 1))
        chains = [(hd, n) for hd in range(PAIR) for n in range(len(js))]
        scores = [_dot(kts[n], qs[hd]) for hd, n in chains]
        log_betas, sps, parts = [], [], []
        for s in scores:
            sp = jnp.maximum(s, 0.0) + jnp.log(1.0 + jnp.exp(-jnp.abs(s)))
            log_betas.append(s - sp)
            if diag:
                sp = jnp.where(strict, sp, 0.0)
            sp_hi = sp.astype(BF16)
            sps.append(sp)
            parts.append((sp_hi, (sp - sp_hi.astype(F32)).astype(BF16)))
        betweens = [_dot(neg_tri, hi) + _dot(neg_tri, lo) for hi, lo in parts]
        carries = [state[hd][0] for hd in range(PAIR)]
        ws = []
        for (hd, n), log_beta, sp, between in zip(chains, log_betas, sps, betweens):
            w = jnp.exp(log_beta + between + carries[hd])
            if diag:
                w = jnp.where(strict, w, 0.0)
            ws.append(w.astype(BF16))
            carries[hd] = carries[hd] - jnp.sum(sp, axis=0, keepdims=True)
        accs = [state[hd][1] for hd in range(PAIR)]
        for (hd, n), w in zip(chains, ws):
            accs[hd] = accs[hd] + _dot(v_ref[0, js[n], _head_rows(hd), :], w)
        return tuple((carries[hd], accs[hd]) for hd in range(PAIR))

    init = tuple((jnp.zeros((1, t), F32), jnp.zeros((HEAD_DIM, t), F32)) for _ in range(PAIR))
    state = _paired_key_loop(i, init, tiles)
    out = jnp.concatenate([acc for _, acc in state], axis=0)
    o_ref[0, 0] = (out * _silu(g_ref[0, 0].astype(F32))).astype(BF16)


def _sb_attention(qvg, k, neg_tri):
    b, ns, _, t = qvg.shape
    s = k.shape[1]
    return pl.pallas_call(
        _sb_kernel,
        out_shape=jax.ShapeDtypeStruct((b, ns, SB_WIDTH, t), BF16),
        grid=(b, SB_HEADS // PAIR, ns),
        in_specs=[*_attn_specs(ns, s, t, ROW_QA, COL_KA, ROW_VA, ROW_GA, True),
                  pl.BlockSpec((t, t), lambda bi, p, i: (0, 0))],
        out_specs=pl.BlockSpec((1, 1, LANES, t), lambda bi, p, i: (bi, i, p, 0)),
        compiler_params=pltpu.CompilerParams(
            dimension_semantics=("parallel", "parallel", "arbitrary"), vmem_limit_bytes=VMEM_LIMIT),
        name="sb_attn",
    )(qvg, k, qvg, qvg, neg_tri)


CA_BLOCKS = CA_LEFT_CHUNKS * CHUNK // ATTN_TILE + 1


def _ca_kernel(q_ref, k0_ref, k1_ref, k2_ref, v0_ref, v1_ref, v2_ref, g_ref, bias_ref, o_ref):
    i = pl.program_id(2)
    q_pair = q_ref[0, 0]
    k_refs = (k0_ref, k1_ref, k2_ref)
    v_refs = (v0_ref, v1_ref, v2_ref)
    qs = [_select_rows(q_pair, hd * HEAD_DIM, (hd + 1) * HEAD_DIM) for hd in range(PAIR)]
    raw = [[_dot(k_refs[d][0], qs[hd]) for d in range(CA_BLOCKS)] for hd in range(PAIR)]
    probs, denoms = [], []
    for hd in range(PAIR):
        scores = []
        for d in range(CA_BLOCKS):
            s = raw[hd][d] + bias_ref[hd, d]
            if d < CA_BLOCKS - 1:
                s = jnp.where(i - (CA_BLOCKS - 1) + d >= 0, s, NEG_INF)
            scores.append(s)
        m = functools.reduce(jnp.maximum, [jnp.max(s, axis=0, keepdims=True) for s in scores])
        ps = [jnp.exp(s - m) for s in scores]
        denoms.append(functools.reduce(jnp.add, [jnp.sum(p, axis=0, keepdims=True) for p in ps]))
        probs.append([p.astype(BF16) for p in ps])
    outs = []
    for hd in range(PAIR):
        acc = functools.reduce(
            jnp.add, [_dot(v_refs[d][0, 0, _head_rows(hd), :], probs[hd][d]) for d in range(CA_BLOCKS)])
        outs.append(acc / denoms[hd])
    out = jnp.concatenate(outs, axis=0)
    o_ref[0, 0] = (out * _silu(g_ref[0, 0].astype(F32))).astype(BF16)


def _ca_attention(qvg, k, bias):
    b, ns, _, t = qvg.shape
    back = CA_BLOCKS - 1
    qspec, gspec = _attn_specs(ns, k.shape[1], t, ROW_QB, COL_KB, ROW_VB, ROW_GB, False)

    def kspec(d):
        return pl.BlockSpec((1, t, LANES),
                            lambda bi, p, i: (bi, jnp.maximum(i - back + d, 0), COL_KB // LANES + p))

    def vspec(d):
        return pl.BlockSpec((1, 1, LANES, t),
                            lambda bi, p, i: (bi, jnp.maximum(i - back + d, 0), ROW_VB // LANES + p, 0))

    return pl.pallas_call(
        _ca_kernel,
        out_shape=jax.ShapeDtypeStruct((b, ns, CA_WIDTH, t), BF16),
        grid=(b, CA_HEADS // PAIR, ns),
        in_specs=[qspec] + [kspec(d) for d in range(CA_BLOCKS)] + [vspec(d) for d in range(CA_BLOCKS)]
        + [gspec, pl.BlockSpec((PAIR, CA_BLOCKS, t, t), lambda bi, p, i: (p, 0, 0, 0))],
        out_specs=pl.BlockSpec((1, 1, LANES, t), lambda bi, p, i: (bi, i, p, 0)),
        compiler_params=pltpu.CompilerParams(
            dimension_semantics=("parallel", "parallel", "arbitrary"), vmem_limit_bytes=VMEM_LIMIT),
        name="ca_attn",
    )(qvg, k, k, k, qvg, qvg, qvg, qvg, bias)


def _ca_bias(rel_table):
    t = ATTN_TILE
    h = rel_table.shape[0]
    jk = np.arange(t)[:, None]
    iq = np.arange(t)[None, :]
    delta = np.arange(-(t - 1), t + 1)
    tiles = []
    for d in range(CA_BLOCKS):
        back = (CA_BLOCKS - 1 - d) * t
        idx = np.clip(back + delta, -MAX_REL, MAX_REL) + MAX_REL
        diag_vals = rel_table.astype(F32)[:, idx]
        flat = jnp.tile(diag_vals, (1, t))[:, :t * (2 * t - 1)]
        toeplitz = flat.reshape(h, t, 2 * t - 1)[:, :, t - 1:]
        lag = back // CHUNK + iq // CHUNK - jk // CHUNK
        valid = (lag >= 0) & (lag <= CA_LEFT_CHUNKS)
        tiles.append(jnp.where(valid[None], toeplitz, NEG_INF))
    return jnp.stack(tiles, axis=1)


def _da_kernel(q_ref, k_ref, v_ref, g_ref, lamv_ref, subg_ref, o_ref, *, lam_init):
    i = pl.program_id(2)
    t = ATTN_TILE
    scale = DA_QKDIM ** -0.5
    q_pair = q_ref[0, 0]
    qs = [_select_rows(q_pair, c * DA_QKDIM, (c + 1) * DA_QKDIM) for c in range(2 * PAIR)]

    def tiles(state, js, diag):
        kts = [_key_tile(k_ref, j) for j in js]
        if diag:
            mask = (lax.shift_right_logical(lax.broadcasted_iota(jnp.int32, (t, t), 0), int(math.log2(CHUNK)))
                    <= lax.shift_right_logical(lax.broadcasted_iota(jnp.int32, (t, t), 1), int(math.log2(CHUNK))))
        maps = range(2 * PAIR)
        scores = [[_dot(kt, qs[c]) * scale for kt in kts] for c in maps]
        if diag:
            scores = [[jnp.where(mask, s, NEG_INF) for s in ss] for ss in scores]
        new_m, new_l, alphas, probs = [], [], [], []
        for c in maps:
            m, l, _ = state[c]
            m_new = functools.reduce(jnp.maximum, [m] + [jnp.max(s, axis=0, keepdims=True) for s in scores[c]])
            alpha = jnp.exp(m - m_new)
            ps = [jnp.exp(s - m_new) for s in scores[c]]
            new_m.append(m_new)
            new_l.append(alpha * l + functools.reduce(jnp.add, [jnp.sum(p, axis=0, keepdims=True) for p in ps]))
            alphas.append(alpha)
            probs.append([p.astype(BF16) for p in ps])
        new_acc = []
        for c in maps:
            acc = alphas[c] * state[c][2]
            for j, p in zip(js, probs[c]):
                acc = acc + _dot(v_ref[0, j, _head_rows(c // 2), :], p)
            new_acc.append(acc)
        return tuple(zip(new_m, new_l, new_acc))

    init = tuple((jnp.full((1, t), NEG_INF, F32), jnp.zeros((1, t), F32), jnp.zeros((HEAD_DIM, t), F32))
                 for _ in range(2 * PAIR))
    state = _paired_key_loop(i, init, tiles)

    lamv = lamv_ref[...]
    lam = (jnp.exp(jnp.sum(lamv[0:1] * lamv[1:2], axis=-1, keepdims=True))
           - jnp.exp(jnp.sum(lamv[2:3] * lamv[3:4], axis=-1, keepdims=True)) + lam_init)
    outs = []
    for hd in range(PAIR):
        (_, l1, a1), (_, l2, a2) = state[2 * hd], state[2 * hd + 1]
        o = a1 / l1 - lam * (a2 / l2)
        o = o * lax.rsqrt(jnp.mean(o * o, axis=0, keepdims=True) + EPS) * subg_ref[...]
        outs.append(o * (1.0 - lam_init))
    out = jnp.concatenate(outs, axis=0)
    o_ref[0, 0] = (out * _silu(g_ref[0, 0].astype(F32))).astype(BF16)


def _da_attention(qvg, k, lamv, subg, lam_init):
    b, ns, _, t = qvg.shape
    s = k.shape[1]
    return pl.pallas_call(
        functools.partial(_da_kernel, lam_init=lam_init),
        out_shape=jax.ShapeDtypeStruct((b, ns, DA_WIDTH, t), BF16),
        grid=(b, DA_HEADS // PAIR, ns),
        in_specs=[*_attn_specs(ns, s, t, ROW_QC, COL_KC, ROW_VC, ROW_GC, True),
                  pl.BlockSpec(lamv.shape, lambda bi, p, i: (0, 0)),
                  pl.BlockSpec(subg.shape, lambda bi, p, i: (0, 0))],
        out_specs=pl.BlockSpec((1, 1, LANES, t), lambda bi, p, i: (bi, i, p, 0)),
        compiler_params=pltpu.CompilerParams(
            dimension_semantics=("parallel", "parallel", "arbitrary"), vmem_limit_bytes=VMEM_LIMIT),
        name="da_attn",
    )(qvg, k, qvg, qvg, lamv, subg)


def _out_proj_kernel(ya_ref, yb_ref, yc_ref, wo_ref, g_ref, x_ref, o_ref):
    t = ATTN_TILE
    for n in range(ya_ref.shape[1]):
        y_in = jnp.concatenate([ya_ref[0, n], yb_ref[0, n], yc_ref[0, n]], axis=0)
        yt = _dot(wo_ref[...], y_in)
        yt = yt * lax.rsqrt(jnp.mean(yt * yt, axis=0, keepdims=True) + EPS)
        rows = slice(n * t, (n + 1) * t)
        o_ref[0, rows, :] = x_ref[0, rows, :] + yt.T * g_ref[...]


def _out_proj(ya, yb, yc, wot, post_g, x):
    b, s, d = x.shape
    ts = min(PROJ_TILE, s)
    nt = ts // ATTN_TILE
    const = lambda *_: (0, 0)
    yspec = lambda w: pl.BlockSpec((1, nt, w, ATTN_TILE), lambda bi, si: (bi, si, 0, 0))
    return pl.pallas_call(
        _out_proj_kernel,
        out_shape=jax.ShapeDtypeStruct((b, s, d), F32),
        grid=(b, s // ts),
        in_specs=[yspec(SB_WIDTH), yspec(CA_WIDTH), yspec(DA_WIDTH),
                  pl.BlockSpec(wot.shape, const),
                  pl.BlockSpec((1, d), const),
                  pl.BlockSpec((1, ts, d), lambda bi, si: (bi, si, 0))],
        out_specs=pl.BlockSpec((1, ts, d), lambda bi, si: (bi, si, 0)),
        compiler_params=pltpu.CompilerParams(
            dimension_semantics=("parallel", "parallel"), vmem_limit_bytes=VMEM_LIMIT),
        name="out_proj",
    )(ya, yb, yc, wot, post_g, x)


def _split_w_in(w_in):
    a, bb, c = SB_WIDTH, CA_WIDTH, DA_WIDTH
    o_b, o_c = 4 * a, 4 * (a + bb)
    col = lambda o, w, n: w_in[:, o + n * w:o + (n + 1) * w]
    qvg = jnp.concatenate([col(0, a, 0), col(0, a, 2), col(0, a, 3),
                           col(o_b, bb, 0), col(o_b, bb, 2), col(o_b, bb, 3),
                           col(o_c, c, 0), col(o_c, c, 2), col(o_c, c, 3)], axis=1)
    wt = qvg.T.astype(BF16)
    wk = jnp.concatenate([col(0, a, 1), col(o_b, bb, 1)], axis=1).astype(BF16)
    wkct = col(o_c, c, 1).T.astype(BF16)
    return wt, wk, wkct


def _layer(x, pos3, invf, neg_tri, w_in, w_out, pre_g, post_g, rel_table, lamv, subln_g, lam_init):
    wt, wk, wkct = _split_w_in(w_in)
    qvg, k = _in_proj(x, pre_g[None, :], wt, wk, wkct, pos3, invf)
    ya = _sb_attention(qvg, k, neg_tri)
    yb = _ca_attention(qvg, k, _ca_bias(rel_table))
    yc = _da_attention(qvg, k, lamv, subln_g[:, None], lam_init)
    return _out_proj(ya, yb, yc, w_out.T.astype(BF16), post_g[None, :], x)


def kernel(x, positions, w_in, w_out, pre_gain, post_gain, rel_bias, lambda_q1, lambda_k1, lambda_q2,
           lambda_k2, subln_gain):
    b, s, d = x.shape
    assert d == D_MODEL and s % PROJ_TILE == 0
    half = DA_QKDIM // 2
    invf = jnp.asarray((ROPE_THETA ** (-np.arange(half, dtype=np.float32) / half))[:, None])
    pos3 = positions.reshape(b, 1, s)
    t = ATTN_TILE
    neg_tri = jnp.asarray(-(np.arange(t)[None, :] > np.arange(t)[:, None]).astype(np.float32), dtype=BF16)
    for layer in range(w_in.shape[0]):
        lam_init = 0.8 - 0.6 * math.exp(-0.3 * layer)
        lamv = jnp.stack([lambda_q1[layer], lambda_k1[layer], lambda_q2[layer], lambda_k2[layer]])
        x = _layer(x, pos3, invf, neg_tri, w_in[layer], w_out[layer], pre_gain[layer], post_gain[layer],
                   rel_bias[layer], lamv, subln_gain[layer], lam_init)
    return x
```

```python
import functools
import math

import numpy as np
import jax
import jax.numpy as jnp
from jax import lax
from jax.experimental import pallas as pl
from jax.experimental.pallas import tpu as pltpu

D_MODEL = 1024
CHUNK = 64
HEAD_DIM = 64
SB_HEADS = 6
CA_HEADS = 6
CA_LEFT_CHUNKS = 8
MAX_REL = 256
DA_HEADS = 4
DA_QKDIM = 32
ROPE_THETA = 10000.0
EPS = 1e-6

SB_WIDTH = SB_HEADS * HEAD_DIM
CA_WIDTH = CA_HEADS * HEAD_DIM
DA_WIDTH = DA_HEADS * HEAD_DIM
QVG_ROWS = 3 * (SB_WIDTH + CA_WIDTH + DA_WIDTH)
K_COLS = SB_WIDTH + CA_WIDTH + DA_WIDTH

ATTN_TILE = 256
PROJ_TILE = 512
LANES = 128
PAIR = LANES // HEAD_DIM
Q_TILES = 2
SB_KEYS_PER_TRIP = 4
DA_KEYS_PER_TRIP = 4
CHAIN_LAG = 2
VMEM_LIMIT = 48 * 1024 * 1024
LOG2E = math.log2(math.e)

ROW_QA, ROW_VA, ROW_GA = 0, SB_WIDTH, 2 * SB_WIDTH
ROW_QB = 3 * SB_WIDTH
ROW_VB, ROW_GB = ROW_QB + CA_WIDTH, ROW_QB + 2 * CA_WIDTH
ROW_QC = ROW_QB + 3 * CA_WIDTH
ROW_VC, ROW_GC = ROW_QC + DA_WIDTH, ROW_QC + 2 * DA_WIDTH
COL_KA, COL_KB, COL_KC = 0, SB_WIDTH, SB_WIDTH + CA_WIDTH

F32 = jnp.float32
BF16 = jnp.bfloat16
NEG_INF = float("-inf")


def _nt_dot(a, b):
    return lax.dot_general(a, b, (((1,), (1,)), ((), ())), preferred_element_type=F32)


def _dot(a, b):
    return jnp.dot(a, b, preferred_element_type=F32)


def _silu(g):
    return g * (1.0 / (1.0 + jnp.exp(-g)))


def _select_rows(q_pair, lo, hi):
    row = lax.broadcasted_iota(jnp.int32, q_pair.shape, 0)
    return jnp.where((row >= lo) & (row < hi), q_pair, jnp.zeros_like(q_pair))


def _head_rows(hd):
    return slice(hd * HEAD_DIM, (hd + 1) * HEAD_DIM)


def _key_tile(k_ref, j):
    return k_ref[0, pl.ds(pl.multiple_of(j * ATTN_TILE, ATTN_TILE), ATTN_TILE), :]


def _causal_key_loop(step, state, tiles, width):
    i0 = Q_TILES * step
    slots = range(Q_TILES)

    def head(r):
        keys = [i0 + 1, i0] + [i0 - 1 - m for m in range(r)]
        work = [(1, 0, True), (1, 1, False), (0, 1, True)] + [(qt, 2 + m, False) for m in range(r) for qt in slots]
        return lambda st: tiles(st, keys, work)

    rems = list(range(0, width, Q_TILES))
    rem = i0 % width
    if len(rems) == 1:
        state = head(0)(state)
    else:
        state = lax.switch(rem // Q_TILES, [head(r) for r in rems], state)
    top = i0 - 1 - rem

    def body(n, st):
        j = top - width * n
        return tiles(st, [j - m for m in range(width)], [(qt, m, False) for m in range(width) for qt in slots])

    return lax.fori_loop(0, i0 // width, body, state)


def _attn_specs(ns, s, t, row_q, col_k, row_v, row_g, q_tiles, resident):
    q = pl.BlockSpec((1, q_tiles, LANES, t), lambda bi, p, i: (bi, i, row_q // LANES + p, 0))
    g = pl.BlockSpec((1, q_tiles, LANES, t), lambda bi, p, i: (bi, i, row_g // LANES + p, 0))
    if resident:
        k = pl.BlockSpec((1, s, LANES), lambda bi, p, i: (bi, 0, col_k // LANES + p))
        v = pl.BlockSpec((1, ns, LANES, t), lambda bi, p, i: (bi, 0, row_v // LANES + p, 0))
        return q, k, v, g
    return q, g


def _in_proj_kernel(x_ref, g_ref, wt_ref, wk_ref, wkct_ref, pos_ref, invf_ref, qvg_ref, k_ref):
    x = x_ref[0]
    ms = jnp.mean(x * x, axis=-1, keepdims=True)
    h = (x * lax.rsqrt(ms + EPS) * g_ref[...]).astype(BF16)
    ts = h.shape[0]

    ang = pos_ref[0].astype(F32) * invf_ref[...]
    cos, sin = jnp.cos(ang), jnp.sin(ang)
    half = DA_QKDIM // 2

    def rotary_t(t):
        parts = []
        for grp in range(DA_WIDTH // DA_QKDIM):
            x1 = t[grp * DA_QKDIM:grp * DA_QKDIM + half]
            x2 = t[grp * DA_QKDIM + half:(grp + 1) * DA_QKDIM]
            parts.append(x1 * cos - x2 * sin)
            parts.append(x2 * cos + x1 * sin)
        return jnp.concatenate(parts, axis=0)

    def store_rows(r0, val):
        for t in range(ts // ATTN_TILE):
            qvg_ref[0, t, r0:r0 + val.shape[0], :] = val[:, t * ATTN_TILE:(t + 1) * ATTN_TILE].astype(BF16)

    chunk = SB_WIDTH
    for r0 in range(0, QVG_ROWS, chunk):
        res = _nt_dot(wt_ref[r0:r0 + chunk, :], h)
        if r0 in (ROW_QA, ROW_QB):
            res = res * (HEAD_DIM ** -0.5)
            store_rows(r0, res)
        elif r0 == ROW_QC:
            store_rows(r0, rotary_t(res[:DA_WIDTH]))
            store_rows(r0 + DA_WIDTH, res[DA_WIDTH:])
        else:
            store_rows(r0, res)

    kab = _dot(h, wk_ref[...])
    k_ref[0, :, 0:COL_KC] = kab.astype(BF16)
    kct = rotary_t(_nt_dot(wkct_ref[...], h))
    k_ref[0, :, COL_KC:K_COLS] = kct.T.astype(BF16)


def _in_proj(x, pre_g, wt, wk, wkct, pos3, invf):
    b, s, d = x.shape
    ts = min(PROJ_TILE, s)
    ns = s // ATTN_TILE
    const = lambda *_: (0, 0)
    return pl.pallas_call(
        _in_proj_kernel,
        out_shape=(jax.ShapeDtypeStruct((b, ns, QVG_ROWS, ATTN_TILE), BF16),
                   jax.ShapeDtypeStruct((b, s, K_COLS), BF16)),
        grid=(b, s // ts),
        in_specs=[
            pl.BlockSpec((1, ts, d), lambda bi, si: (bi, si, 0)),
            pl.BlockSpec((1, d), const),
            pl.BlockSpec(wt.shape, const),
            pl.BlockSpec(wk.shape, const),
            pl.BlockSpec(wkct.shape, const),
            pl.BlockSpec((1, 1, ts), lambda bi, si: (bi, 0, si)),
            pl.BlockSpec(invf.shape, const),
        ],
        out_specs=(
            pl.BlockSpec((1, ts // ATTN_TILE, QVG_ROWS, ATTN_TILE), lambda bi, si: (bi, si, 0, 0)),
            pl.BlockSpec((1, ts, K_COLS), lambda bi, si: (bi, si, 0)),
        ),
        compiler_params=pltpu.CompilerParams(
            dimension_semantics=("parallel", "parallel"), vmem_limit_bytes=VMEM_LIMIT),
        name="in_proj",
    )(x, pre_g, wt, wk, wkct, pos3, invf)


def _sb_kernel(q_ref, k_ref, v_ref, g_ref, tri_ref, o_ref):
    t = ATTN_TILE
    heads = range(PAIR)
    qs = [[_select_rows(q_ref[0, qt], hd * HEAD_DIM, (hd + 1) * HEAD_DIM) for hd in heads]
          for qt in range(Q_TILES)]
    neg_tri = tri_ref[...]

    def tiles(state, keys, work):
        kts = [_key_tile(k_ref, j) for j in keys]
        strict = (lax.broadcasted_iota(jnp.int32, (t, t), 0)
                  < lax.broadcasted_iota(jnp.int32, (t, t), 1))
        chains = [(qt, hd, kpos, diag) for qt, kpos, diag in work for hd in heads]
        carries = [[state[qt][hd][0] for hd in heads] for qt in range(Q_TILES)]
        accs = [[state[qt][hd][1] for hd in heads] for qt in range(Q_TILES)]
        n_chains = len(chains)
        scores, tails = [None] * n_chains, [None] * n_chains

        def score(n):
            qt, hd, kpos, diag = chains[n]
            s = _dot(kts[kpos], qs[qt][hd])
            if diag:
                s = jnp.where(strict, s, NEG_INF)
            scores[n] = s

        def tail(n):
            s = scores[n]
            sp = (jnp.maximum(s, 0.0) + jnp.log(1.0 + jnp.exp2(jnp.abs(s) * -LOG2E))).astype(BF16)
            tails[n] = _dot(neg_tri, sp)

        def weigh(n):
            qt, hd, kpos, _ = chains[n]
            w = jnp.exp(scores[n] + tails[n]).astype(BF16)
            carry = carries[qt][hd]
            pv = _dot(v_ref[0, keys[kpos], _head_rows(hd), :], w)
            accs[qt][hd] = accs[qt][hd] + pv * jnp.exp(carry)
            carries[qt][hd] = carry + tails[n][0:1]

        for n in range(n_chains + 2 * CHAIN_LAG):
            if n < n_chains:
                score(n)
            if 0 <= n - CHAIN_LAG < n_chains:
                tail(n - CHAIN_LAG)
            if 0 <= n - 2 * CHAIN_LAG < n_chains:
                weigh(n - 2 * CHAIN_LAG)
        return tuple(tuple((carries[qt][hd], accs[qt][hd]) for hd in heads) for qt in range(Q_TILES))

    init = tuple(tuple((jnp.zeros((1, t), F32), jnp.zeros((HEAD_DIM, t), F32)) for _ in heads)
                 for _ in range(Q_TILES))
    state = _causal_key_loop(pl.program_id(2), init, tiles, SB_KEYS_PER_TRIP)
    for qt in range(Q_TILES):
        out = jnp.concatenate([acc for _, acc in state[qt]], axis=0)
        o_ref[0, qt] = (out * _silu(g_ref[0, qt].astype(F32))).astype(BF16)


def _sb_attention(qvg, k, neg_tri):
    b, ns, _, t = qvg.shape
    s = k.shape[1]
    return pl.pallas_call(
        _sb_kernel,
        out_shape=jax.ShapeDtypeStruct((b, ns, SB_WIDTH, t), BF16),
        grid=(b, SB_HEADS // PAIR, ns // Q_TILES),
        in_specs=[*_attn_specs(ns, s, t, ROW_QA, COL_KA, ROW_VA, ROW_GA, Q_TILES, True),
                  pl.BlockSpec((t, t), lambda bi, p, i: (0, 0))],
        out_specs=pl.BlockSpec((1, Q_TILES, LANES, t), lambda bi, p, i: (bi, i, p, 0)),
        compiler_params=pltpu.CompilerParams(
            dimension_semantics=("parallel", "parallel", "arbitrary"), vmem_limit_bytes=VMEM_LIMIT),
        name="sb_attn",
    )(qvg, k, qvg, qvg, neg_tri)


CA_BLOCKS = CA_LEFT_CHUNKS * CHUNK // ATTN_TILE + 1


def _ca_kernel(q_ref, k0_ref, k1_ref, k2_ref, v0_ref, v1_ref, v2_ref, g_ref, bias_ref, o_ref):
    i = pl.program_id(2)
    q_pair = q_ref[0, 0]
    k_refs = (k0_ref, k1_ref, k2_ref)
    v_refs = (v0_ref, v1_ref, v2_ref)
    qs = [_select_rows(q_pair, hd * HEAD_DIM, (hd + 1) * HEAD_DIM) for hd in range(PAIR)]
    raw = [[_dot(k_refs[d][0], qs[hd]) for d in range(CA_BLOCKS)] for hd in range(PAIR)]
    probs, denoms = [], []
    for hd in range(PAIR):
        scores = []
        for d in range(CA_BLOCKS):
            s = raw[hd][d] + bias_ref[hd, d]
            if d < CA_BLOCKS - 1:
                s = jnp.where(i - (CA_BLOCKS - 1) + d >= 0, s, NEG_INF)
            scores.append(s)
        m = functools.reduce(jnp.maximum, [jnp.max(s, axis=0, keepdims=True) for s in scores])
        ps = [jnp.exp(s - m) for s in scores]
        denoms.append(functools.reduce(jnp.add, [jnp.sum(p, axis=0, keepdims=True) for p in ps]))
        probs.append([p.astype(BF16) for p in ps])
    outs = []
    for hd in range(PAIR):
        acc = functools.reduce(
            jnp.add, [_dot(v_refs[d][0, 0, _head_rows(hd), :], probs[hd][d]) for d in range(CA_BLOCKS)])
        outs.append(acc / denoms[hd])
    out = jnp.concatenate(outs, axis=0)
    o_ref[0, 0] = (out * _silu(g_ref[0, 0].astype(F32))).astype(BF16)


def _ca_attention(qvg, k, bias):
    b, ns, _, t = qvg.shape
    back = CA_BLOCKS - 1
    qspec, gspec = _attn_specs(ns, k.shape[1], t, ROW_QB, COL_KB, ROW_VB, ROW_GB, 1, False)

    def kspec(d):
        return pl.BlockSpec((1, t, LANES),
                            lambda bi, p, i: (bi, jnp.maximum(i - back + d, 0), COL_KB // LANES + p))

    def vspec(d):
        return pl.BlockSpec((1, 1, LANES, t),
                            lambda bi, p, i: (bi, jnp.maximum(i - back + d, 0), ROW_VB // LANES + p, 0))

    return pl.pallas_call(
        _ca_kernel,
        out_shape=jax.ShapeDtypeStruct((b, ns, CA_WIDTH, t), BF16),
        grid=(b, CA_HEADS // PAIR, ns),
        in_specs=[qspec] + [kspec(d) for d in range(CA_BLOCKS)] + [vspec(d) for d in range(CA_BLOCKS)]
        + [gspec, pl.BlockSpec((PAIR, CA_BLOCKS, t, t), lambda bi, p, i: (p, 0, 0, 0))],
        out_specs=pl.BlockSpec((1, 1, LANES, t), lambda bi, p, i: (bi, i, p, 0)),
        compiler_params=pltpu.CompilerParams(
            dimension_semantics=("parallel", "parallel", "arbitrary"), vmem_limit_bytes=VMEM_LIMIT),
        name="ca_attn",
    )(qvg, k, k, k, qvg, qvg, qvg, qvg, bias)


def _ca_bias(rel_table):
    t = ATTN_TILE
    h = rel_table.shape[0]
    jk = np.arange(t)[:, None]
    iq = np.arange(t)[None, :]
    delta = np.arange(-(t - 1), t + 1)
    tiles = []
    for d in range(CA_BLOCKS):
        back = (CA_BLOCKS - 1 - d) * t
        idx = np.clip(back + delta, -MAX_REL, MAX_REL) + MAX_REL
        diag_vals = rel_table.astype(F32)[:, idx]
        flat = jnp.tile(diag_vals, (1, t))[:, :t * (2 * t - 1)]
        toeplitz = flat.reshape(h, t, 2 * t - 1)[:, :, t - 1:]
        lag = back // CHUNK + iq // CHUNK - jk // CHUNK
        valid = (lag >= 0) & (lag <= CA_LEFT_CHUNKS)
        tiles.append(jnp.where(valid[None], toeplitz, NEG_INF))
    return jnp.stack(tiles, axis=1)


def _da_kernel(q_ref, k_ref, v_ref, g_ref, lamv_ref, subg_ref, o_ref, *, lam_init):
    t = ATTN_TILE
    exp2_scale = DA_QKDIM ** -0.5 * LOG2E
    maps = range(2 * PAIR)
    qs = [[_select_rows(q_ref[0, qt], c * DA_QKDIM, (c + 1) * DA_QKDIM) for c in maps] for qt in range(Q_TILES)]
    chunk_shift = int(math.log2(CHUNK))

    def tiles(state, keys, work):
        kts = [_key_tile(k_ref, j) for j in keys]
        mask = (lax.shift_right_logical(lax.broadcasted_iota(jnp.int32, (t, t), 0), chunk_shift)
                <= lax.shift_right_logical(lax.broadcasted_iota(jnp.int32, (t, t), 1), chunk_shift))
        groups = [(qt, c) for qt in range(Q_TILES) for c in maps]
        scores, stats, probs, new = {}, {}, {}, {}

        def score(qt, c):
            scores[qt, c] = []
            for wq, kpos, diag in work:
                if wq == qt:
                    s = _dot(kts[kpos], qs[qt][c])
                    if diag:
                        s = jnp.where(mask, s, NEG_INF)
                    scores[qt, c].append((kpos, s))

        def normalise(qt, c):
            m, l, _ = state[qt][c]
            m_new = functools.reduce(jnp.maximum,
                                     [m] + [jnp.max(s, axis=0, keepdims=True) for _, s in scores[qt, c]])
            alpha = jnp.exp2((m - m_new) * exp2_scale)
            ps = [(kpos, jnp.exp2((s - m_new) * exp2_scale)) for kpos, s in scores[qt, c]]
            l_new = alpha * l + functools.reduce(jnp.add, [jnp.sum(p, axis=0, keepdims=True) for _, p in ps])
            stats[qt, c] = (m_new, l_new, alpha)
            probs[qt, c] = [(kpos, p.astype(BF16)) for kpos, p in ps]

        def accumulate(qt, c):
            m_new, l_new, alpha = stats[qt, c]
            acc = alpha * state[qt][c][2]
            for kpos, p in probs[qt, c]:
                acc = acc + _dot(v_ref[0, keys[kpos], _head_rows(c // 2), :], p)
            new[qt, c] = (m_new, l_new, acc)

        for stage in (score, normalise, accumulate):
            for qt, c in groups:
                stage(qt, c)
        return tuple(tuple(new[qt, c] for c in maps) for qt in range(Q_TILES))

    init = tuple(tuple((jnp.full((1, t), NEG_INF, F32), jnp.zeros((1, t), F32), jnp.zeros((HEAD_DIM, t), F32))
                       for _ in maps) for _ in range(Q_TILES))
    state = _causal_key_loop(pl.program_id(2), init, tiles, DA_KEYS_PER_TRIP)

    lamv = lamv_ref[...]
    lam = (jnp.exp(jnp.sum(lamv[0:1] * lamv[1:2], axis=-1, keepdims=True))
           - jnp.exp(jnp.sum(lamv[2:3] * lamv[3:4], axis=-1, keepdims=True)) + lam_init)
    for qt in range(Q_TILES):
        outs = []
        for hd in range(PAIR):
            (_, l1, a1), (_, l2, a2) = state[qt][2 * hd], state[qt][2 * hd + 1]
            o = a1 / l1 - lam * (a2 / l2)
            o = o * lax.rsqrt(jnp.mean(o * o, axis=0, keepdims=True) + EPS) * subg_ref[...]
            outs.append(o * (1.0 - lam_init))
        out = jnp.concatenate(outs, axis=0)
        o_ref[0, qt] = (out * _silu(g_ref[0, qt].astype(F32))).astype(BF16)


def _da_attention(qvg, k, lamv, subg, lam_init):
    b, ns, _, t = qvg.shape
    s = k.shape[1]
    return pl.pallas_call(
        functools.partial(_da_kernel, lam_init=lam_init),
        out_shape=jax.ShapeDtypeStruct((b, ns, DA_WIDTH, t), BF16),
        grid=(b, DA_HEADS // PAIR, ns // Q_TILES),
        in_specs=[*_attn_specs(ns, s, t, ROW_QC, COL_KC, ROW_VC, ROW_GC, Q_TILES, True),
                  pl.BlockSpec(lamv.shape, lambda bi, p, i: (0, 0)),
                  pl.BlockSpec(subg.shape, lambda bi, p, i: (0, 0))],
        out_specs=pl.BlockSpec((1, Q_TILES, LANES, t), lambda bi, p, i: (bi, i, p, 0)),
        compiler_params=pltpu.CompilerParams(
            dimension_semantics=("parallel", "parallel", "arbitrary"), vmem_limit_bytes=VMEM_LIMIT),
        name="da_attn",
    )(qvg, k, qvg, qvg, lamv, subg)


def _out_proj_kernel(ya_ref, yb_ref, yc_ref, wo_ref, g_ref, x_ref, o_ref):
    t = ATTN_TILE
    for n in range(ya_ref.shape[1]):
        y_in = jnp.concatenate([ya_ref[0, n], yb_ref[0, n], yc_ref[0, n]], axis=0)
        yt = _dot(wo_ref[...], y_in)
        yt = yt * lax.rsqrt(jnp.mean(yt * yt, axis=0, keepdims=True) + EPS)
        rows = slice(n * t, (n + 1) * t)
        o_ref[0, rows, :] = x_ref[0, rows, :] + yt.T * g_ref[...]


def _out_proj(ya, yb, yc, wot, post_g, x):
    b, s, d = x.shape
    ts = min(PROJ_TILE, s)
    nt = ts // ATTN_TILE
    const = lambda *_: (0, 0)
    yspec = lambda w: pl.BlockSpec((1, nt, w, ATTN_TILE), lambda bi, si: (bi, si, 0, 0))
    return pl.pallas_call(
        _out_proj_kernel,
        out_shape=jax.ShapeDtypeStruct((b, s, d), F32),
        grid=(b, s // ts),
        in_specs=[yspec(SB_WIDTH), yspec(CA_WIDTH), yspec(DA_WIDTH),
                  pl.BlockSpec(wot.shape, const),
                  pl.BlockSpec((1, d), const),
                  pl.BlockSpec((1, ts, d), lambda bi, si: (bi, si, 0))],
        out_specs=pl.BlockSpec((1, ts, d), lambda bi, si: (bi, si, 0)),
        compiler_params=pltpu.CompilerParams(
            dimension_semantics=("parallel", "parallel"), vmem_limit_bytes=VMEM_LIMIT),
        name="out_proj",
    )(ya, yb, yc, wot, post_g, x)


def _split_w_in(w_in):
    a, bb, c = SB_WIDTH, CA_WIDTH, DA_WIDTH
    o_b, o_c = 4 * a, 4 * (a + bb)
    col = lambda o, w, n: w_in[:, o + n * w:o + (n + 1) * w]
    qvg = jnp.concatenate([col(0, a, 0), col(0, a, 2), col(0, a, 3),
                           col(o_b, bb, 0), col(o_b, bb, 2), col(o_b, bb, 3),
                           col(o_c, c, 0), col(o_c, c, 2), col(o_c, c, 3)], axis=1)
    wt = qvg.T.astype(BF16)
    wk = jnp.concatenate([col(0, a, 1), col(o_b, bb, 1)], axis=1).astype(BF16)
    wkct = col(o_c, c, 1).T.astype(BF16)
    return wt, wk, wkct


def _layer(x, pos3, invf, neg_tri, w_in, w_out, pre_g, post_g, rel_table, lamv, subln_g, lam_init):
    wt, wk, wkct = _split_w_in(w_in)
    qvg, k = _in_proj(x, pre_g[None, :], wt, wk, wkct, pos3, invf)
    ya = _sb_attention(qvg, k, neg_tri)
    yb = _ca_attention(qvg, k, _ca_bias(rel_table))
    yc = _da_attention(qvg, k, lamv, subln_g[:, None], lam_init)
    return _out_proj(ya, yb, yc, w_out.T.astype(BF16), post_g[None, :], x)


def kernel(x, positions, w_in, w_out, pre_gain, post_gain, rel_bias, lambda_q1, lambda_k1, lambda_q2,
           lambda_k2, subln_gain):
    b, s, d = x.shape
    assert d == D_MODEL and s % PROJ_TILE == 0
    half = DA_QKDIM // 2
    invf = jnp.asarray((ROPE_THETA ** (-np.arange(half, dtype=np.float32) / half))[:, None])
    pos3 = positions.reshape(b, 1, s)
    t = ATTN_TILE
    neg_tri = jnp.asarray(-(np.arange(t)[None, :] >= np.arange(t)[:, None]).astype(np.float32), dtype=BF16)
    for layer in range(w_in.shape[0]):
        lam_init = 0.8 - 0.6 * math.exp(-0.3 * layer)
        lamv = jnp.stack([lambda_q1[layer], lambda_k1[layer], lambda_q2[layer], lambda_k2[layer]])
        x = _layer(x, pos3, invf, neg_tri, w_in[layer], w_out[layer], pre_gain[layer], post_gain[layer],
                   rel_bias[layer], lamv, subln_gain[layer], lam_init)
    return x
```

```python
import functools
import math

import numpy as np
import jax
import jax.numpy as jnp
from jax import lax
from jax.experimental import pallas as pl
from jax.experimental.pallas import tpu as pltpu

D_MODEL = 1024
CHUNK = 64
HEAD_DIM = 64
SB_HEADS = 6
CA_HEADS = 6
CA_LEFT_CHUNKS = 8
MAX_REL = 256
DA_HEADS = 4
DA_QKDIM = 32
ROPE_THETA = 10000.0
EPS = 1e-6

SB_WIDTH = SB_HEADS * HEAD_DIM
CA_WIDTH = CA_HEADS * HEAD_DIM
DA_WIDTH = DA_HEADS * HEAD_DIM
QVG_ROWS = 3 * (SB_WIDTH + CA_WIDTH + DA_WIDTH)
K_COLS = SB_WIDTH + CA_WIDTH + DA_WIDTH

ATTN_TILE = 256
PROJ_TILE = 512
LANES = 128
PAIR = LANES // HEAD_DIM
ONES_ROWS = 16
Q_TILES = 2
SB_KEYS_PER_TRIP = 4
DA_KEYS_PER_TRIP = 4
CHAIN_LAG = 2
VMEM_LIMIT = 48 * 1024 * 1024
LOG2E = math.log2(math.e)
DA_EXP2_SCALE = DA_QKDIM ** -0.5 * LOG2E

ROW_QA, ROW_VA, ROW_GA = 0, SB_WIDTH, 2 * SB_WIDTH
ROW_QB = 3 * SB_WIDTH
ROW_VB, ROW_GB = ROW_QB + CA_WIDTH, ROW_QB + 2 * CA_WIDTH
ROW_QC = ROW_QB + 3 * CA_WIDTH
ROW_VC, ROW_GC = ROW_QC + DA_WIDTH, ROW_QC + 2 * DA_WIDTH
COL_KA, COL_KB, COL_KC = 0, SB_WIDTH, SB_WIDTH + CA_WIDTH

F32 = jnp.float32
BF16 = jnp.bfloat16
NEG_INF = float("-inf")


def _nt_dot(a, b):
    return lax.dot_general(a, b, (((1,), (1,)), ((), ())), preferred_element_type=F32)


def _dot(a, b):
    return jnp.dot(a, b, preferred_element_type=F32)


def _silu(g):
    return g * (1.0 / (1.0 + jnp.exp(-g)))


def _select_rows(q_pair, lo, hi):
    row = lax.broadcasted_iota(jnp.int32, q_pair.shape, 0)
    return jnp.where((row >= lo) & (row < hi), q_pair, jnp.zeros_like(q_pair))


def _head_rows(hd):
    return slice(hd * HEAD_DIM, (hd + 1) * HEAD_DIM)


def _with_ones_rows(v):
    return jnp.concatenate([v, jnp.ones((ONES_ROWS, v.shape[1]), v.dtype)], axis=0)


def _key_tile(k_ref, j):
    return k_ref[0, pl.ds(pl.multiple_of(j * ATTN_TILE, ATTN_TILE), ATTN_TILE), :]


def _causal_key_loop(step, state, tiles, width):
    i0 = Q_TILES * step
    slots = range(Q_TILES)

    def head(r):
        keys = [i0 + 1, i0] + [i0 - 1 - m for m in range(r)]
        work = [(1, 0, True), (1, 1, False), (0, 1, True)] + [(qt, 2 + m, False) for m in range(r) for qt in slots]
        return lambda st: tiles(st, keys, work)

    rems = list(range(0, width, Q_TILES))
    rem = i0 % width
    if len(rems) == 1:
        state = head(0)(state)
    else:
        state = lax.switch(rem // Q_TILES, [head(r) for r in rems], state)
    top = i0 - 1 - rem

    def body(n, st):
        j = top - width * n
        return tiles(st, [j - m for m in range(width)], [(qt, m, False) for m in range(width) for qt in slots])

    return lax.fori_loop(0, i0 // width, body, state)


def _attn_specs(ns, s, t, row_q, col_k, row_v, row_g, q_tiles, resident):
    q = pl.BlockSpec((1, q_tiles, LANES, t), lambda bi, p, i: (bi, i, row_q // LANES + p, 0))
    g = pl.BlockSpec((1, q_tiles, LANES, t), lambda bi, p, i: (bi, i, row_g // LANES + p, 0))
    if resident:
        k = pl.BlockSpec((1, s, LANES), lambda bi, p, i: (bi, 0, col_k // LANES + p))
        v = pl.BlockSpec((1, ns, LANES, t), lambda bi, p, i: (bi, 0, row_v // LANES + p, 0))
        return q, k, v, g
    return q, g


def _in_proj_kernel(x_ref, g_ref, wt_ref, wk_ref, wkct_ref, pos_ref, invf_ref, qvg_ref, k_ref):
    x = x_ref[0]
    ms = jnp.mean(x * x, axis=-1, keepdims=True)
    h = (x * lax.rsqrt(ms + EPS) * g_ref[...]).astype(BF16)
    ts = h.shape[0]

    ang = pos_ref[0].astype(F32) * invf_ref[...]
    cos, sin = jnp.cos(ang), jnp.sin(ang)
    half = DA_QKDIM // 2

    def rotary_t(t):
        parts = []
        for grp in range(DA_WIDTH // DA_QKDIM):
            x1 = t[grp * DA_QKDIM:grp * DA_QKDIM + half]
            x2 = t[grp * DA_QKDIM + half:(grp + 1) * DA_QKDIM]
            parts.append(x1 * cos - x2 * sin)
            parts.append(x2 * cos + x1 * sin)
        return jnp.concatenate(parts, axis=0)

    def store_rows(r0, val):
        for t in range(ts // ATTN_TILE):
            qvg_ref[0, t, r0:r0 + val.shape[0], :] = val[:, t * ATTN_TILE:(t + 1) * ATTN_TILE].astype(BF16)

    chunk = SB_WIDTH
    for r0 in range(0, QVG_ROWS, chunk):
        res = _nt_dot(wt_ref[r0:r0 + chunk, :], h)
        if r0 == ROW_QA:
            store_rows(r0, res * (HEAD_DIM ** -0.5))
        elif r0 == ROW_QB:
            store_rows(r0, res * (HEAD_DIM ** -0.5 * LOG2E))
        elif r0 == ROW_QC:
            store_rows(r0, rotary_t(res[:DA_WIDTH]) * DA_EXP2_SCALE)
            store_rows(r0 + DA_WIDTH, res[DA_WIDTH:])
        else:
            store_rows(r0, res)

    kab = _dot(h, wk_ref[...])
    k_ref[0, :, 0:COL_KC] = kab.astype(BF16)
    kct = rotary_t(_nt_dot(wkct_ref[...], h))
    k_ref[0, :, COL_KC:K_COLS] = kct.T.astype(BF16)


def _in_proj(x, pre_g, wt, wk, wkct, pos3, invf):
    b, s, d = x.shape
    ts = min(PROJ_TILE, s)
    ns = s // ATTN_TILE
    const = lambda *_: (0, 0)
    return pl.pallas_call(
        _in_proj_kernel,
        out_shape=(jax.ShapeDtypeStruct((b, ns, QVG_ROWS, ATTN_TILE), BF16),
                   jax.ShapeDtypeStruct((b, s, K_COLS), BF16)),
        grid=(b, s // ts),
        in_specs=[
            pl.BlockSpec((1, ts, d), lambda bi, si: (bi, si, 0)),
            pl.BlockSpec((1, d), const),
            pl.BlockSpec(wt.shape, const),
            pl.BlockSpec(wk.shape, const),
            pl.BlockSpec(wkct.shape, const),
            pl.BlockSpec((1, 1, ts), lambda bi, si: (bi, 0, si)),
            pl.BlockSpec(invf.shape, const),
        ],
        out_specs=(
            pl.BlockSpec((1, ts // ATTN_TILE, QVG_ROWS, ATTN_TILE), lambda bi, si: (bi, si, 0, 0)),
            pl.BlockSpec((1, ts, K_COLS), lambda bi, si: (bi, si, 0)),
        ),
        compiler_params=pltpu.CompilerParams(
            dimension_semantics=("parallel", "parallel"), vmem_limit_bytes=VMEM_LIMIT),
        name="in_proj",
    )(x, pre_g, wt, wk, wkct, pos3, invf)


def _sb_kernel(q_ref, k_ref, v_ref, g_ref, tri_ref, o_ref):
    t = ATTN_TILE
    heads = range(PAIR)
    qs = [[_select_rows(q_ref[0, qt], hd * HEAD_DIM, (hd + 1) * HEAD_DIM) for hd in heads]
          for qt in range(Q_TILES)]
    neg_tri = tri_ref[...]

    def tiles(state, keys, work):
        kts = [_key_tile(k_ref, j) for j in keys]
        strict = (lax.broadcasted_iota(jnp.int32, (t, t), 0)
                  < lax.broadcasted_iota(jnp.int32, (t, t), 1))
        chains = [(qt, hd, kpos, diag) for qt, kpos, diag in work for hd in heads]
        carries = [[state[qt][hd][0] for hd in heads] for qt in range(Q_TILES)]
        accs = [[state[qt][hd][1] for hd in heads] for qt in range(Q_TILES)]
        n_chains = len(chains)
        scores, tails = [None] * n_chains, [None] * n_chains

        def score(n):
            qt, hd, kpos, diag = chains[n]
            s = _dot(kts[kpos], qs[qt][hd])
            if diag:
                s = jnp.where(strict, s, NEG_INF)
            scores[n] = s

        def tail(n):
            s = scores[n]
            sp = (jnp.maximum(s, 0.0) + jnp.log(1.0 + jnp.exp2(jnp.abs(s) * -LOG2E))).astype(BF16)
            tails[n] = _dot(neg_tri, sp)

        def weigh(n):
            qt, hd, kpos, _ = chains[n]
            w = jnp.exp(scores[n] + tails[n]).astype(BF16)
            carry = carries[qt][hd]
            pv = _dot(v_ref[0, keys[kpos], _head_rows(hd), :], w)
            accs[qt][hd] = accs[qt][hd] + pv * jnp.exp(carry)
            carries[qt][hd] = carry + tails[n][0:1]

        for n in range(n_chains + 2 * CHAIN_LAG):
            if n < n_chains:
                score(n)
            if 0 <= n - CHAIN_LAG < n_chains:
                tail(n - CHAIN_LAG)
            if 0 <= n - 2 * CHAIN_LAG < n_chains:
                weigh(n - 2 * CHAIN_LAG)
        return tuple(tuple((carries[qt][hd], accs[qt][hd]) for hd in heads) for qt in range(Q_TILES))

    init = tuple(tuple((jnp.zeros((1, t), F32), jnp.zeros((HEAD_DIM, t), F32)) for _ in heads)
                 for _ in range(Q_TILES))
    state = _causal_key_loop(pl.program_id(2), init, tiles, SB_KEYS_PER_TRIP)
    for qt in range(Q_TILES):
        out = jnp.concatenate([acc for _, acc in state[qt]], axis=0)
        o_ref[0, qt] = (out * _silu(g_ref[0, qt].astype(F32))).astype(BF16)


def _sb_attention(qvg, k, neg_tri):
    b, ns, _, t = qvg.shape
    s = k.shape[1]
    return pl.pallas_call(
        _sb_kernel,
        out_shape=jax.ShapeDtypeStruct((b, ns, SB_WIDTH, t), BF16),
        grid=(b, SB_HEADS // PAIR, ns // Q_TILES),
        in_specs=[*_attn_specs(ns, s, t, ROW_QA, COL_KA, ROW_VA, ROW_GA, Q_TILES, True),
                  pl.BlockSpec((t, t), lambda bi, p, i: (0, 0))],
        out_specs=pl.BlockSpec((1, Q_TILES, LANES, t), lambda bi, p, i: (bi, i, p, 0)),
        compiler_params=pltpu.CompilerParams(
            dimension_semantics=("parallel", "parallel", "arbitrary"), vmem_limit_bytes=VMEM_LIMIT),
        name="sb_attn",
    )(qvg, k, qvg, qvg, neg_tri)


CA_BLOCKS = CA_LEFT_CHUNKS * CHUNK // ATTN_TILE + 1


CA_KEY_BLOCKS = CA_BLOCKS - 1 + Q_TILES


def _ca_kernel(q_ref, *refs):
    k_refs, v_refs = refs[:CA_KEY_BLOCKS], refs[CA_KEY_BLOCKS:2 * CA_KEY_BLOCKS]
    g_ref, bias_ref, o_ref = refs[2 * CA_KEY_BLOCKS:]
    back = CA_BLOCKS - 1
    started = pl.program_id(2) > 0
    groups = [(qt, hd) for qt in range(Q_TILES) for hd in range(PAIR)]
    scores, probs, outs = {}, {}, {}

    def score(qt, hd):
        q = _select_rows(q_ref[0, qt], hd * HEAD_DIM, (hd + 1) * HEAD_DIM)
        scores[qt, hd] = []
        for d in range(CA_BLOCKS):
            s = _dot(k_refs[qt + d][0], q) + bias_ref[hd, d]
            if qt + d < back:
                s = jnp.where(started, s, NEG_INF)
            scores[qt, hd].append(s)

    def normalise(qt, hd):
        ss = scores[qt, hd]
        m = functools.reduce(jnp.maximum, [jnp.max(s, axis=0, keepdims=True) for s in ss])
        probs[qt, hd] = [jnp.exp2(s - m).astype(BF16) for s in ss]

    def accumulate(qt, hd):
        acc = functools.reduce(jnp.add, [_dot(_with_ones_rows(v_refs[qt + d][0, 0, _head_rows(hd), :]),
                                              probs[qt, hd][d]) for d in range(CA_BLOCKS)])
        outs[qt, hd] = acc[:HEAD_DIM] / acc[HEAD_DIM:HEAD_DIM + 1]

    for stage in (score, normalise, accumulate):
        for qt, hd in groups:
            stage(qt, hd)
    for qt in range(Q_TILES):
        out = jnp.concatenate([outs[qt, hd] for hd in range(PAIR)], axis=0)
        o_ref[0, qt] = (out * _silu(g_ref[0, qt].astype(F32))).astype(BF16)


def _ca_attention(qvg, k, bias):
    b, ns, _, t = qvg.shape
    back = CA_BLOCKS - 1
    qspec, gspec = _attn_specs(ns, k.shape[1], t, ROW_QB, COL_KB, ROW_VB, ROW_GB, Q_TILES, False)

    def kspec(n):
        return pl.BlockSpec((1, t, LANES),
                            lambda bi, p, i: (bi, jnp.maximum(Q_TILES * i - back + n, 0), COL_KB // LANES + p))

    def vspec(n):
        return pl.BlockSpec((1, 1, LANES, t),
                            lambda bi, p, i: (bi, jnp.maximum(Q_TILES * i - back + n, 0), ROW_VB // LANES + p, 0))

    return pl.pallas_call(
        _ca_kernel,
        out_shape=jax.ShapeDtypeStruct((b, ns, CA_WIDTH, t), BF16),
        grid=(b, CA_HEADS // PAIR, ns // Q_TILES),
        in_specs=[qspec] + [kspec(n) for n in range(CA_KEY_BLOCKS)] + [vspec(n) for n in range(CA_KEY_BLOCKS)]
        + [gspec, pl.BlockSpec((PAIR, CA_BLOCKS, t, t), lambda bi, p, i: (p, 0, 0, 0))],
        out_specs=pl.BlockSpec((1, Q_TILES, LANES, t), lambda bi, p, i: (bi, i, p, 0)),
        compiler_params=pltpu.CompilerParams(
            dimension_semantics=("parallel", "parallel", "arbitrary"), vmem_limit_bytes=VMEM_LIMIT),
        name="ca_attn",
    )(qvg, *[k] * CA_KEY_BLOCKS, *[qvg] * CA_KEY_BLOCKS, qvg, bias)


def _ca_bias(rel_table):
    t = ATTN_TILE
    h = rel_table.shape[0]
    jk = np.arange(t)[:, None]
    iq = np.arange(t)[None, :]
    delta = np.arange(-(t - 1), t + 1)
    tiles = []
    for d in range(CA_BLOCKS):
        back = (CA_BLOCKS - 1 - d) * t
        idx = np.clip(back + delta, -MAX_REL, MAX_REL) + MAX_REL
        diag_vals = rel_table.astype(F32)[:, idx]
        flat = jnp.tile(diag_vals, (1, t))[:, :t * (2 * t - 1)]
        toeplitz = flat.reshape(h, t, 2 * t - 1)[:, :, t - 1:]
        lag = back // CHUNK + iq // CHUNK - jk // CHUNK
        valid = (lag >= 0) & (lag <= CA_LEFT_CHUNKS)
        tiles.append(jnp.where(valid[None], toeplitz * LOG2E, NEG_INF))
    return jnp.stack(tiles, axis=1)


def _da_kernel(q_ref, k_ref, v_ref, g_ref, lamv_ref, subg_ref, o_ref, *, lam_init):
    t = ATTN_TILE
    maps = range(2 * PAIR)
    qs = [[_select_rows(q_ref[0, qt], c * DA_QKDIM, (c + 1) * DA_QKDIM) for c in maps] for qt in range(Q_TILES)]
    chunk_shift = int(math.log2(CHUNK))

    def tiles(state, keys, work):
        kts = [_key_tile(k_ref, j) for j in keys]
        mask = (lax.shift_right_logical(lax.broadcasted_iota(jnp.int32, (t, t), 0), chunk_shift)
                <= lax.shift_right_logical(lax.broadcasted_iota(jnp.int32, (t, t), 1), chunk_shift))
        groups = [(qt, c) for qt in range(Q_TILES) for c in maps]
        scores, stats, probs, new = {}, {}, {}, {}

        def score(qt, c):
            scores[qt, c] = []
            for wq, kpos, diag in work:
                if wq == qt:
                    s = _dot(kts[kpos], qs[qt][c])
                    if diag:
                        s = jnp.where(mask, s, NEG_INF)
                    scores[qt, c].append((kpos, s))

        def normalise(qt, c):
            m = state[qt][c][0]
            m_new = functools.reduce(jnp.maximum,
                                     [m] + [jnp.max(s, axis=0, keepdims=True) for _, s in scores[qt, c]])
            stats[qt, c] = (m_new, jnp.exp2(m - m_new))
            probs[qt, c] = [(kpos, jnp.exp2(s - m_new).astype(BF16)) for kpos, s in scores[qt, c]]

        def accumulate(qt, c):
            m_new, alpha = stats[qt, c]
            acc = alpha * state[qt][c][1]
            for kpos, p in probs[qt, c]:
                acc = acc + _dot(_with_ones_rows(v_ref[0, keys[kpos], _head_rows(c // 2), :]), p)
            new[qt, c] = (m_new, acc)

        for stage in (score, normalise, accumulate):
            for qt, c in groups:
                stage(qt, c)
        return tuple(tuple(new[qt, c] for c in maps) for qt in range(Q_TILES))

    init = tuple(tuple((jnp.full((1, t), NEG_INF, F32), jnp.zeros((HEAD_DIM + ONES_ROWS, t), F32))
                       for _ in maps) for _ in range(Q_TILES))
    state = _causal_key_loop(pl.program_id(2), init, tiles, DA_KEYS_PER_TRIP)

    lamv = lamv_ref[...]
    lam = (jnp.exp(jnp.sum(lamv[0:1] * lamv[1:2], axis=-1, keepdims=True))
           - jnp.exp(jnp.sum(lamv[2:3] * lamv[3:4], axis=-1, keepdims=True)) + lam_init)
    for qt in range(Q_TILES):
        outs = []
        for hd in range(PAIR):
            (_, acc1), (_, acc2) = state[qt][2 * hd], state[qt][2 * hd + 1]
            o = (acc1[:HEAD_DIM] / acc1[HEAD_DIM:HEAD_DIM + 1]
                 - lam * (acc2[:HEAD_DIM] / acc2[HEAD_DIM:HEAD_DIM + 1]))
            o = o * lax.rsqrt(jnp.mean(o * o, axis=0, keepdims=True) + EPS) * subg_ref[...]
            outs.append(o * (1.0 - lam_init))
        out = jnp.concatenate(outs, axis=0)
        o_ref[0, qt] = (out * _silu(g_ref[0, qt].astype(F32))).astype(BF16)


def _da_attention(qvg, k, lamv, subg, lam_init):
    b, ns, _, t = qvg.shape
    s = k.shape[1]
    return pl.pallas_call(
        functools.partial(_da_kernel, lam_init=lam_init),
        out_shape=jax.ShapeDtypeStruct((b, ns, DA_WIDTH, t), BF16),
        grid=(b, DA_HEADS // PAIR, ns // Q_TILES),
        in_specs=[*_attn_specs(ns, s, t, ROW_QC, COL_KC, ROW_VC, ROW_GC, Q_TILES, True),
                  pl.BlockSpec(lamv.shape, lambda bi, p, i: (0, 0)),
                  pl.BlockSpec(subg.shape, lambda bi, p, i: (0, 0))],
        out_specs=pl.BlockSpec((1, Q_TILES, LANES, t), lambda bi, p, i: (bi, i, p, 0)),
        compiler_params=pltpu.CompilerParams(
            dimension_semantics=("parallel", "parallel", "arbitrary"), vmem_limit_bytes=VMEM_LIMIT),
        name="da_attn",
    )(qvg, k, qvg, qvg, lamv, subg)


def _out_proj_kernel(ya_ref, yb_ref, yc_ref, wo_ref, g_ref, x_ref, o_ref):
    t = ATTN_TILE
    for n in range(ya_ref.shape[1]):
        y_in = jnp.concatenate([ya_ref[0, n], yb_ref[0, n], yc_ref[0, n]], axis=0)
        yt = _dot(wo_ref[...], y_in)
        yt = yt * lax.rsqrt(jnp.mean(yt * yt, axis=0, keepdims=True) + EPS)
        rows = slice(n * t, (n + 1) * t)
        o_ref[0, rows, :] = x_ref[0, rows, :] + yt.T * g_ref[...]


def _out_proj(ya, yb, yc, wot, post_g, x):
    b, s, d = x.shape
    ts = min(PROJ_TILE, s)
    nt = ts // ATTN_TILE
    const = lambda *_: (0, 0)
    yspec = lambda w: pl.BlockSpec((1, nt, w, ATTN_TILE), lambda bi, si: (bi, si, 0, 0))
    return pl.pallas_call(
        _out_proj_kernel,
        out_shape=jax.ShapeDtypeStruct((b, s, d), F32),
        grid=(b, s // ts),
        in_specs=[yspec(SB_WIDTH), yspec(CA_WIDTH), yspec(DA_WIDTH),
                  pl.BlockSpec(wot.shape, const),
                  pl.BlockSpec((1, d), const),
                  pl.BlockSpec((1, ts, d), lambda bi, si: (bi, si, 0))],
        out_specs=pl.BlockSpec((1, ts, d), lambda bi, si: (bi, si, 0)),
        compiler_params=pltpu.CompilerParams(
            dimension_semantics=("parallel", "parallel"), vmem_limit_bytes=VMEM_LIMIT),
        name="out_proj",
    )(ya, yb, yc, wot, post_g, x)


def _split_w_in(w_in):
    a, bb, c = SB_WIDTH, CA_WIDTH, DA_WIDTH
    o_b, o_c = 4 * a, 4 * (a + bb)
    col = lambda o, w, n: w_in[:, o + n * w:o + (n + 1) * w]
    qvg = jnp.concatenate([col(0, a, 0), col(0, a, 2), col(0, a, 3),
                           col(o_b, bb, 0), col(o_b, bb, 2), col(o_b, bb, 3),
                           col(o_c, c, 0), col(o_c, c, 2), col(o_c, c, 3)], axis=1)
    wt = qvg.T.astype(BF16)
    wk = jnp.concatenate([col(0, a, 1), col(o_b, bb, 1)], axis=1).astype(BF16)
    wkct = col(o_c, c, 1).T.astype(BF16)
    return wt, wk, wkct


def _layer(x, pos3, invf, neg_tri, w_in, w_out, pre_g, post_g, rel_table, lamv, subln_g, lam_init):
    wt, wk, wkct = _split_w_in(w_in)
    qvg, k = _in_proj(x, pre_g[None, :], wt, wk, wkct, pos3, invf)
    ya = _sb_attention(qvg, k, neg_tri)
    yb = _ca_attention(qvg, k, _ca_bias(rel_table))
    yc = _da_attention(qvg, k, lamv, subln_g[:, None], lam_init)
    return _out_proj(ya, yb, yc, w_out.T.astype(BF16), post_g[None, :], x)


def kernel(x, positions, w_in, w_out, pre_gain, post_gain, rel_bias, lambda_q1, lambda_k1, lambda_q2,
           lambda_k2, subln_gain):
    b, s, d = x.shape
    assert d == D_MODEL and s % PROJ_TILE == 0
    half = DA_QKDIM // 2
    invf = jnp.asarray((ROPE_THETA ** (-np.arange(half, dtype=np.float32) / half))[:, None])
    pos3 = positions.reshape(b, 1, s)
    t = ATTN_TILE
    neg_tri = jnp.asarray(-(np.arange(t)[None, :] >= np.arange(t)[:, None]).astype(np.float32), dtype=BF16)
    for layer in range(w_in.shape[0]):
        lam_init = 0.8 - 0.6 * math.exp(-0.3 * layer)
        lamv = jnp.stack([lambda_q1[layer], lambda_k1[layer], lambda_q2[layer], lambda_k2[layer]])
        x = _layer(x, pos3, invf, neg_tri, w_in[layer], w_out[layer], pre_gain[layer], post_gain[layer],
                   rel_bias[layer], lamv, subln_gain[layer], lam_init)
    return x
```

```python
import functools
import math

import numpy as np
import jax
import jax.numpy as jnp
from jax import lax
from jax.experimental import pallas as pl
from jax.experimental.pallas import tpu as pltpu

D_MODEL = 1024
CHUNK = 64
HEAD_DIM = 64
SB_HEADS = 6
CA_HEADS = 6
CA_LEFT_CHUNKS = 8
MAX_REL = 256
DA_HEADS = 4
DA_QKDIM = 32
ROPE_THETA = 10000.0
EPS = 1e-6

SB_WIDTH = SB_HEADS * HEAD_DIM
CA_WIDTH = CA_HEADS * HEAD_DIM
DA_WIDTH = DA_HEADS * HEAD_DIM
QVG_ROWS = 3 * (SB_WIDTH + CA_WIDTH + DA_WIDTH)
K_COLS = SB_WIDTH + CA_WIDTH + DA_WIDTH

ATTN_TILE = 256
PROJ_TILE = 512
LANES = 128
PAIR = LANES // HEAD_DIM
ONES_ROWS = 16
Q_TILES = 2
SB_KEYS_PER_TRIP = 4
DA_KEYS_PER_TRIP = 4
CHAIN_LAG = 2
VMEM_LIMIT = 48 * 1024 * 1024
LOG2E = math.log2(math.e)
SB_DEAD_CARRY = -160.0 / LOG2E
DA_EXP2_SCALE = DA_QKDIM ** -0.5 * LOG2E

ROW_QA, ROW_VA, ROW_GA = 0, SB_WIDTH, 2 * SB_WIDTH
ROW_QB = 3 * SB_WIDTH
ROW_VB, ROW_GB = ROW_QB + CA_WIDTH, ROW_QB + 2 * CA_WIDTH
ROW_QC = ROW_QB + 3 * CA_WIDTH
ROW_VC, ROW_GC = ROW_QC + DA_WIDTH, ROW_QC + 2 * DA_WIDTH
COL_KA, COL_KB, COL_KC = 0, SB_WIDTH, SB_WIDTH + CA_WIDTH

F32 = jnp.float32
BF16 = jnp.bfloat16
NEG_INF = float("-inf")


def _nt_dot(a, b):
    return lax.dot_general(a, b, (((1,), (1,)), ((), ())), preferred_element_type=F32)


def _dot(a, b):
    return jnp.dot(a, b, preferred_element_type=F32)


def _silu(g):
    return g * (1.0 / (1.0 + jnp.exp(-g)))


def _select_rows(q_pair, lo, hi):
    row = lax.broadcasted_iota(jnp.int32, q_pair.shape, 0)
    return jnp.where((row >= lo) & (row < hi), q_pair, jnp.zeros_like(q_pair))


def _head_rows(hd):
    return slice(hd * HEAD_DIM, (hd + 1) * HEAD_DIM)


def _with_ones_rows(v):
    return jnp.concatenate([v, jnp.ones((ONES_ROWS, v.shape[1]), v.dtype)], axis=0)


def _key_tile(k_ref, j):
    return k_ref[0, pl.ds(pl.multiple_of(j * ATTN_TILE, ATTN_TILE), ATTN_TILE), :]


def _causal_key_loop(step, state, tiles, width, live=None):
    i0 = Q_TILES * step
    slots = range(Q_TILES)

    def head(r):
        keys = [i0 + 1, i0] + [i0 - 1 - m for m in range(r)]
        work = [(1, 0, True), (1, 1, False), (0, 1, True)] + [(qt, 2 + m, False) for m in range(r) for qt in slots]
        return lambda st: tiles(st, keys, work)

    rems = list(range(0, width, Q_TILES))
    rem = i0 % width
    if len(rems) == 1:
        state = head(0)(state)
    else:
        state = lax.switch(rem // Q_TILES, [head(r) for r in rems], state)
    top = i0 - 1 - rem

    def body(n, st):
        j = top - width * n
        return tiles(st, [j - m for m in range(width)], [(qt, m, False) for m in range(width) for qt in slots])

    trips = i0 // width
    if live is None:
        return lax.fori_loop(0, trips, body, state)
    return lax.while_loop(lambda c: (c[0] < trips) & live(c[1]), lambda c: (c[0] + 1, body(c[0], c[1])),
                          (jnp.int32(0), state))[1]


def _attn_specs(ns, s, t, row_q, col_k, row_v, row_g, q_tiles, resident):
    q = pl.BlockSpec((1, q_tiles, LANES, t), lambda bi, p, i: (bi, i, row_q // LANES + p, 0))
    g = pl.BlockSpec((1, q_tiles, LANES, t), lambda bi, p, i: (bi, i, row_g // LANES + p, 0))
    if resident:
        k = pl.BlockSpec((1, s, LANES), lambda bi, p, i: (bi, 0, col_k // LANES + p))
        v = pl.BlockSpec((1, ns, LANES, t), lambda bi, p, i: (bi, 0, row_v // LANES + p, 0))
        return q, k, v, g
    return q, g


def _in_proj_kernel(x_ref, g_ref, wt_ref, wk_ref, wkct_ref, pos_ref, invf_ref, qvg_ref, k_ref):
    x = x_ref[0]
    ms = jnp.mean(x * x, axis=-1, keepdims=True)
    h = (x * lax.rsqrt(ms + EPS) * g_ref[...]).astype(BF16)
    ts = h.shape[0]

    ang = pos_ref[0].astype(F32) * invf_ref[...]
    cos, sin = jnp.cos(ang), jnp.sin(ang)
    half = DA_QKDIM // 2

    def rotary_t(t):
        parts = []
        for grp in range(DA_WIDTH // DA_QKDIM):
            x1 = t[grp * DA_QKDIM:grp * DA_QKDIM + half]
            x2 = t[grp * DA_QKDIM + half:(grp + 1) * DA_QKDIM]
            parts.append(x1 * cos - x2 * sin)
            parts.append(x2 * cos + x1 * sin)
        return jnp.concatenate(parts, axis=0)

    def store_rows(r0, val):
        for t in range(ts // ATTN_TILE):
            qvg_ref[0, t, r0:r0 + val.shape[0], :] = val[:, t * ATTN_TILE:(t + 1) * ATTN_TILE].astype(BF16)

    chunk = SB_WIDTH
    for r0 in range(0, QVG_ROWS, chunk):
        res = _nt_dot(wt_ref[r0:r0 + chunk, :], h)
        if r0 == ROW_QA:
            store_rows(r0, res * (HEAD_DIM ** -0.5))
        elif r0 == ROW_QB:
            store_rows(r0, res * (HEAD_DIM ** -0.5 * LOG2E))
        elif r0 == ROW_QC:
            store_rows(r0, rotary_t(res[:DA_WIDTH]) * DA_EXP2_SCALE)
            store_rows(r0 + DA_WIDTH, res[DA_WIDTH:])
        else:
            store_rows(r0, res)

    kab = _dot(h, wk_ref[...])
    k_ref[0, :, 0:COL_KC] = kab.astype(BF16)
    kct = rotary_t(_nt_dot(wkct_ref[...], h))
    k_ref[0, :, COL_KC:K_COLS] = kct.T.astype(BF16)


def _in_proj(x, pre_g, wt, wk, wkct, pos3, invf):
    b, s, d = x.shape
    ts = min(PROJ_TILE, s)
    ns = s // ATTN_TILE
    const = lambda *_: (0, 0)
    return pl.pallas_call(
        _in_proj_kernel,
        out_shape=(jax.ShapeDtypeStruct((b, ns, QVG_ROWS, ATTN_TILE), BF16),
                   jax.ShapeDtypeStruct((b, s, K_COLS), BF16)),
        grid=(b, s // ts),
        in_specs=[
            pl.BlockSpec((1, ts, d), lambda bi, si: (bi, si, 0)),
            pl.BlockSpec((1, d), const),
            pl.BlockSpec(wt.shape, const),
            pl.BlockSpec(wk.shape, const),
            pl.BlockSpec(wkct.shape, const),
            pl.BlockSpec((1, 1, ts), lambda bi, si: (bi, 0, si)),
            pl.BlockSpec(invf.shape, const),
        ],
        out_specs=(
            pl.BlockSpec((1, ts // ATTN_TILE, QVG_ROWS, ATTN_TILE), lambda bi, si: (bi, si, 0, 0)),
            pl.BlockSpec((1, ts, K_COLS), lambda bi, si: (bi, si, 0)),
        ),
        compiler_params=pltpu.CompilerParams(
            dimension_semantics=("parallel", "parallel"), vmem_limit_bytes=VMEM_LIMIT),
        name="in_proj",
    )(x, pre_g, wt, wk, wkct, pos3, invf)


def _sb_kernel(q_ref, k_ref, v_ref, g_ref, tri_ref, o_ref):
    t = ATTN_TILE
    heads = range(PAIR)
    qs = [[_select_rows(q_ref[0, qt], hd * HEAD_DIM, (hd + 1) * HEAD_DIM) for hd in heads]
          for qt in range(Q_TILES)]
    neg_tri = tri_ref[...]

    def tiles(state, keys, work):
        kts = [_key_tile(k_ref, j) for j in keys]
        strict = (lax.broadcasted_iota(jnp.int32, (t, t), 0)
                  < lax.broadcasted_iota(jnp.int32, (t, t), 1))
        chains = [(qt, hd, kpos, diag) for qt, kpos, diag in work for hd in heads]
        carries = [[state[qt][hd][0] for hd in heads] for qt in range(Q_TILES)]
        accs = [[state[qt][hd][1] for hd in heads] for qt in range(Q_TILES)]
        n_chains = len(chains)
        scores, tails = [None] * n_chains, [None] * n_chains

        def score(n):
            qt, hd, kpos, diag = chains[n]
            s = _dot(kts[kpos], qs[qt][hd])
            if diag:
                s = jnp.where(strict, s, NEG_INF)
            scores[n] = s

        def tail(n):
            s = scores[n]
            sp = (jnp.maximum(s, 0.0) + jnp.log(1.0 + jnp.exp2(jnp.abs(s) * -LOG2E))).astype(BF16)
            tails[n] = _dot(neg_tri, sp)

        def weigh(n):
            qt, hd, kpos, _ = chains[n]
            w = jnp.exp(scores[n] + tails[n]).astype(BF16)
            carry = carries[qt][hd]
            pv = _dot(v_ref[0, keys[kpos], _head_rows(hd), :], w)
            accs[qt][hd] = accs[qt][hd] + pv * jnp.exp(carry)
            carries[qt][hd] = carry + tails[n][0:1]

        for n in range(n_chains + 2 * CHAIN_LAG):
            if n < n_chains:
                score(n)
            if 0 <= n - CHAIN_LAG < n_chains:
                tail(n - CHAIN_LAG)
            if 0 <= n - 2 * CHAIN_LAG < n_chains:
                weigh(n - 2 * CHAIN_LAG)
        return tuple(tuple((carries[qt][hd], accs[qt][hd]) for hd in heads) for qt in range(Q_TILES))

    init = tuple(tuple((jnp.zeros((1, t), F32), jnp.zeros((HEAD_DIM, t), F32)) for _ in heads)
                 for _ in range(Q_TILES))

    def live(state):
        carry = functools.reduce(jnp.maximum, [state[qt][hd][0] for qt in range(Q_TILES) for hd in heads])
        return jnp.max(carry) > SB_DEAD_CARRY

    state = _causal_key_loop(pl.program_id(2), init, tiles, SB_KEYS_PER_TRIP, live)
    for qt in range(Q_TILES):
        out = jnp.concatenate([acc for _, acc in state[qt]], axis=0)
        o_ref[0, qt] = (out * _silu(g_ref[0, qt].astype(F32))).astype(BF16)


def _sb_attention(qvg, k, neg_tri):
    b, ns, _, t = qvg.shape
    s = k.shape[1]
    return pl.pallas_call(
        _sb_kernel,
        out_shape=jax.ShapeDtypeStruct((b, ns, SB_WIDTH, t), BF16),
        grid=(b, SB_HEADS // PAIR, ns // Q_TILES),
        in_specs=[*_attn_specs(ns, s, t, ROW_QA, COL_KA, ROW_VA, ROW_GA, Q_TILES, True),
                  pl.BlockSpec((t, t), lambda bi, p, i: (0, 0))],
        out_specs=pl.BlockSpec((1, Q_TILES, LANES, t), lambda bi, p, i: (bi, i, p, 0)),
        compiler_params=pltpu.CompilerParams(
            dimension_semantics=("parallel", "parallel", "arbitrary"), vmem_limit_bytes=VMEM_LIMIT),
        name="sb_attn",
    )(qvg, k, qvg, qvg, neg_tri)


CA_BLOCKS = CA_LEFT_CHUNKS * CHUNK // ATTN_TILE + 1


CA_Q_TILES = 4
CA_KEY_BLOCKS = CA_BLOCKS - 1 + CA_Q_TILES


def _ca_kernel(q_ref, *refs):
    k_refs, v_refs = refs[:CA_KEY_BLOCKS], refs[CA_KEY_BLOCKS:2 * CA_KEY_BLOCKS]
    g_ref, bias_ref, o_ref = refs[2 * CA_KEY_BLOCKS:]
    back = CA_BLOCKS - 1
    started = pl.program_id(2) > 0
    groups = [(qt, hd) for qt in range(CA_Q_TILES) for hd in range(PAIR)]
    scores, probs, outs = {}, {}, {}

    def score(qt, hd):
        q = _select_rows(q_ref[0, qt], hd * HEAD_DIM, (hd + 1) * HEAD_DIM)
        scores[qt, hd] = []
        for d in range(CA_BLOCKS):
            s = _dot(k_refs[qt + d][0], q) + bias_ref[hd, d]
            if qt + d < back:
                s = jnp.where(started, s, NEG_INF)
            scores[qt, hd].append(s)

    def normalise(qt, hd):
        ss = scores[qt, hd]
        m = functools.reduce(jnp.maximum, [jnp.max(s, axis=0, keepdims=True) for s in ss])
        probs[qt, hd] = [jnp.exp2(s - m).astype(BF16) for s in ss]

    def accumulate(qt, hd):
        acc = functools.reduce(jnp.add, [_dot(_with_ones_rows(v_refs[qt + d][0, 0, _head_rows(hd), :]),
                                              probs[qt, hd][d]) for d in range(CA_BLOCKS)])
        outs[qt, hd] = acc[:HEAD_DIM] / acc[HEAD_DIM:HEAD_DIM + 1]

    for stage in (score, normalise, accumulate):
        for qt, hd in groups:
            stage(qt, hd)
    for qt in range(CA_Q_TILES):
        out = jnp.concatenate([outs[qt, hd] for hd in range(PAIR)], axis=0)
        o_ref[0, qt] = (out * _silu(g_ref[0, qt].astype(F32))).astype(BF16)


def _ca_attention(qvg, k, bias):
    b, ns, _, t = qvg.shape
    back = CA_BLOCKS - 1
    qspec, gspec = _attn_specs(ns, k.shape[1], t, ROW_QB, COL_KB, ROW_VB, ROW_GB, CA_Q_TILES, False)

    def kspec(n):
        return pl.BlockSpec((1, t, LANES),
                            lambda bi, p, i: (bi, jnp.maximum(CA_Q_TILES * i - back + n, 0), COL_KB // LANES + p))

    def vspec(n):
        return pl.BlockSpec((1, 1, LANES, t),
                            lambda bi, p, i: (bi, jnp.maximum(CA_Q_TILES * i - back + n, 0), ROW_VB // LANES + p, 0))

    return pl.pallas_call(
        _ca_kernel,
        out_shape=jax.ShapeDtypeStruct((b, ns, CA_WIDTH, t), BF16),
        grid=(b, CA_HEADS // PAIR, ns // CA_Q_TILES),
        in_specs=[qspec] + [kspec(n) for n in range(CA_KEY_BLOCKS)] + [vspec(n) for n in range(CA_KEY_BLOCKS)]
        + [gspec, pl.BlockSpec((PAIR, CA_BLOCKS, t, t), lambda bi, p, i: (p, 0, 0, 0))],
        out_specs=pl.BlockSpec((1, CA_Q_TILES, LANES, t), lambda bi, p, i: (bi, i, p, 0)),
        compiler_params=pltpu.CompilerParams(
            dimension_semantics=("parallel", "parallel", "arbitrary"), vmem_limit_bytes=VMEM_LIMIT),
        name="ca_attn",
    )(qvg, *[k] * CA_KEY_BLOCKS, *[qvg] * CA_KEY_BLOCKS, qvg, bias)


def _ca_bias(rel_table):
    t = ATTN_TILE
    h = rel_table.shape[0]
    jk = np.arange(t)[:, None]
    iq = np.arange(t)[None, :]
    delta = np.arange(-(t - 1), t + 1)
    tiles = []
    for d in range(CA_BLOCKS):
        back = (CA_BLOCKS - 1 - d) * t
        idx = np.clip(back + delta, -MAX_REL, MAX_REL) + MAX_REL
        diag_vals = rel_table.astype(F32)[:, idx]
        flat = jnp.tile(diag_vals, (1, t))[:, :t * (2 * t - 1)]
        toeplitz = flat.reshape(h, t, 2 * t - 1)[:, :, t - 1:]
        lag = back // CHUNK + iq // CHUNK - jk // CHUNK
        valid = (lag >= 0) & (lag <= CA_LEFT_CHUNKS)
        tiles.append(jnp.where(valid[None], toeplitz * LOG2E, NEG_INF))
    return jnp.stack(tiles, axis=1)


def _da_kernel(q_ref, k_ref, v_ref, g_ref, lamv_ref, subg_ref, o_ref, *, lam_init):
    t = ATTN_TILE
    maps = range(2 * PAIR)
    qs = [[_select_rows(q_ref[0, qt], c * DA_QKDIM, (c + 1) * DA_QKDIM) for c in maps] for qt in range(Q_TILES)]
    chunk_shift = int(math.log2(CHUNK))

    def tiles(state, keys, work):
        kts = [_key_tile(k_ref, j) for j in keys]
        mask = (lax.shift_right_logical(lax.broadcasted_iota(jnp.int32, (t, t), 0), chunk_shift)
                <= lax.shift_right_logical(lax.broadcasted_iota(jnp.int32, (t, t), 1), chunk_shift))
        groups = [(qt, c) for qt in range(Q_TILES) for c in maps]
        scores, stats, probs, new = {}, {}, {}, {}

        def score(qt, c):
            scores[qt, c] = []
            for wq, kpos, diag in work:
                if wq == qt:
                    s = _dot(kts[kpos], qs[qt][c])
                    if diag:
                        s = jnp.where(mask, s, NEG_INF)
                    scores[qt, c].append((kpos, s))

        def normalise(qt, c):
            m = state[qt][c][0]
            m_new = functools.reduce(jnp.maximum,
                                     [m] + [jnp.max(s, axis=0, keepdims=True) for _, s in scores[qt, c]])
            stats[qt, c] = (m_new, jnp.exp2(m - m_new))
            probs[qt, c] = [(kpos, jnp.exp2(s - m_new).astype(BF16)) for kpos, s in scores[qt, c]]

        def accumulate(qt, c):
            m_new, alpha = stats[qt, c]
            acc = alpha * state[qt][c][1]
            for kpos, p in probs[qt, c]:
                acc = acc + _dot(_with_ones_rows(v_ref[0, keys[kpos], _head_rows(c // 2), :]), p)
            new[qt, c] = (m_new, acc)

        for stage in (score, normalise, accumulate):
            for qt, c in groups:
                stage(qt, c)
        return tuple(tuple(new[qt, c] for c in maps) for qt in range(Q_TILES))

    init = tuple(tuple((jnp.full((1, t), NEG_INF, F32), jnp.zeros((HEAD_DIM + ONES_ROWS, t), F32))
                       for _ in maps) for _ in range(Q_TILES))
    state = _causal_key_loop(pl.program_id(2), init, tiles, DA_KEYS_PER_TRIP)

    lamv = lamv_ref[...]
    lam = (jnp.exp(jnp.sum(lamv[0:1] * lamv[1:2], axis=-1, keepdims=True))
           - jnp.exp(jnp.sum(lamv[2:3] * lamv[3:4], axis=-1, keepdims=True)) + lam_init)
    for qt in range(Q_TILES):
        outs = []
        for hd in range(PAIR):
            (_, acc1), (_, acc2) = state[qt][2 * hd], state[qt][2 * hd + 1]
            o = (acc1[:HEAD_DIM] / acc1[HEAD_DIM:HEAD_DIM + 1]
                 - lam * (acc2[:HEAD_DIM] / acc2[HEAD_DIM:HEAD_DIM + 1]))
            o = o * lax.rsqrt(jnp.mean(o * o, axis=0, keepdims=True) + EPS) * subg_ref[...]
            outs.append(o * (1.0 - lam_init))
        out = jnp.concatenate(outs, axis=0)
        o_ref[0, qt] = (out * _silu(g_ref[0, qt].astype(F32))).astype(BF16)


def _da_attention(qvg, k, lamv, subg, lam_init):
    b, ns, _, t = qvg.shape
    s = k.shape[1]
    return pl.pallas_call(
        functools.partial(_da_kernel, lam_init=lam_init),
        out_shape=jax.ShapeDtypeStruct((b, ns, DA_WIDTH, t), BF16),
        grid=(b, DA_HEADS // PAIR, ns // Q_TILES),
        in_specs=[*_attn_specs(ns, s, t, ROW_QC, COL_KC, ROW_VC, ROW_GC, Q_TILES, True),
                  pl.BlockSpec(lamv.shape, lambda bi, p, i: (0, 0)),
                  pl.BlockSpec(subg.shape, lambda bi, p, i: (0, 0))],
        out_specs=pl.BlockSpec((1, Q_TILES, LANES, t), lambda bi, p, i: (bi, i, p, 0)),
        compiler_params=pltpu.CompilerParams(
            dimension_semantics=("parallel", "parallel", "arbitrary"), vmem_limit_bytes=VMEM_LIMIT),
        name="da_attn",
    )(qvg, k, qvg, qvg, lamv, subg)


def _out_proj_kernel(ya_ref, yb_ref, yc_ref, wo_ref, g_ref, x_ref, o_ref):
    t = ATTN_TILE
    for n in range(ya_ref.shape[1]):
        y_in = jnp.concatenate([ya_ref[0, n], yb_ref[0, n], yc_ref[0, n]], axis=0)
        yt = _dot(wo_ref[...], y_in)
        yt = yt * lax.rsqrt(jnp.mean(yt * yt, axis=0, keepdims=True) + EPS)
        rows = slice(n * t, (n + 1) * t)
        o_ref[0, rows, :] = x_ref[0, rows, :] + yt.T * g_ref[...]


def _out_proj(ya, yb, yc, wot, post_g, x):
    b, s, d = x.shape
    ts = min(PROJ_TILE, s)
    nt = ts // ATTN_TILE
    const = lambda *_: (0, 0)
    yspec = lambda w: pl.BlockSpec((1, nt, w, ATTN_TILE), lambda bi, si: (bi, si, 0, 0))
    return pl.pallas_call(
        _out_proj_kernel,
        out_shape=jax.ShapeDtypeStruct((b, s, d), F32),
        grid=(b, s // ts),
        in_specs=[yspec(SB_WIDTH), yspec(CA_WIDTH), yspec(DA_WIDTH),
                  pl.BlockSpec(wot.shape, const),
                  pl.BlockSpec((1, d), const),
                  pl.BlockSpec((1, ts, d), lambda bi, si: (bi, si, 0))],
        out_specs=pl.BlockSpec((1, ts, d), lambda bi, si: (bi, si, 0)),
        compiler_params=pltpu.CompilerParams(
            dimension_semantics=("parallel", "parallel"), vmem_limit_bytes=VMEM_LIMIT),
        name="out_proj",
    )(ya, yb, yc, wot, post_g, x)


def _split_w_in(w_in):
    a, bb, c = SB_WIDTH, CA_WIDTH, DA_WIDTH
    o_b, o_c = 4 * a, 4 * (a + bb)
    col = lambda o, w, n: w_in[:, o + n * w:o + (n + 1) * w]
    qvg = jnp.concatenate([col(0, a, 0), col(0, a, 2), col(0, a, 3),
                           col(o_b, bb, 0), col(o_b, bb, 2), col(o_b, bb, 3),
                           col(o_c, c, 0), col(o_c, c, 2), col(o_c, c, 3)], axis=1)
    wt = qvg.T.astype(BF16)
    wk = jnp.concatenate([col(0, a, 1), col(o_b, bb, 1)], axis=1).astype(BF16)
    wkct = col(o_c, c, 1).T.astype(BF16)
    return wt, wk, wkct


def _layer(x, pos3, invf, neg_tri, w_in, w_out, pre_g, post_g, rel_table, lamv, subln_g, lam_init):
    wt, wk, wkct = _split_w_in(w_in)
    qvg, k = _in_proj(x, pre_g[None, :], wt, wk, wkct, pos3, invf)
    ya = _sb_attention(qvg, k, neg_tri)
    yb = _ca_attention(qvg, k, _ca_bias(rel_table))
    yc = _da_attention(qvg, k, lamv, subln_g[:, None], lam_init)
    return _out_proj(ya, yb, yc, w_out.T.astype(BF16), post_g[None, :], x)


def kernel(x, positions, w_in, w_out, pre_gain, post_gain, rel_bias, lambda_q1, lambda_k1, lambda_q2,
           lambda_k2, subln_gain):
    b, s, d = x.shape
    assert d == D_MODEL and s % PROJ_TILE == 0
    half = DA_QKDIM // 2
    invf = jnp.asarray((ROPE_THETA ** (-np.arange(half, dtype=np.float32) / half))[:, None])
    pos3 = positions.reshape(b, 1, s)
    t = ATTN_TILE
    neg_tri = jnp.asarray(-(np.arange(t)[None, :] >= np.arange(t)[:, None]).astype(np.float32), dtype=BF16)
    for layer in range(w_in.shape[0]):
        lam_init = 0.8 - 0.6 * math.exp(-0.3 * layer)
        lamv = jnp.stack([lambda_q1[layer], lambda_k1[layer], lambda_q2[layer], lambda_k2[layer]])
        x = _layer(x, pos3, invf, neg_tri, w_in[layer], w_out[layer], pre_gain[layer], post_gain[layer],
                   rel_bias[layer], lamv, subln_gain[layer], lam_init)
    return x
```

```python
import functools
import math

import numpy as np
import jax
import jax.numpy as jnp
from jax import lax
from jax.experimental import pallas as pl
from jax.experimental.pallas import tpu as pltpu

D_MODEL = 1024
CHUNK = 64
HEAD_DIM = 64
SB_HEADS = 6
CA_HEADS = 6
CA_LEFT_CHUNKS = 8
MAX_REL = 256
DA_HEADS = 4
DA_QKDIM = 32
ROPE_THETA = 10000.0
EPS = 1e-6

SB_WIDTH = SB_HEADS * HEAD_DIM
CA_WIDTH = CA_HEADS * HEAD_DIM
DA_WIDTH = DA_HEADS * HEAD_DIM
QVG_ROWS = 3 * (SB_WIDTH + CA_WIDTH + DA_WIDTH)
K_COLS = SB_WIDTH + CA_WIDTH + DA_WIDTH

ATTN_TILE = 256
PROJ_TILE = 512
LANES = 128
PAIR = LANES // HEAD_DIM
ONES_ROWS = 16
Q_TILES = 2
SB_KEYS_PER_TRIP = 2
DA_KEYS_PER_TRIP = 4
CHAIN_LAG = 2
VMEM_LIMIT = 48 * 1024 * 1024
LOG2E = math.log2(math.e)
SB_DEAD_CARRY = -160.0 / LOG2E
DA_EXP2_SCALE = DA_QKDIM ** -0.5 * LOG2E

ROW_QA, ROW_VA, ROW_GA = 0, SB_WIDTH, 2 * SB_WIDTH
ROW_QB = 3 * SB_WIDTH
ROW_VB, ROW_GB = ROW_QB + CA_WIDTH, ROW_QB + 2 * CA_WIDTH
ROW_QC = ROW_QB + 3 * CA_WIDTH
ROW_VC, ROW_GC = ROW_QC + DA_WIDTH, ROW_QC + 2 * DA_WIDTH
COL_KA, COL_KB, COL_KC = 0, SB_WIDTH, SB_WIDTH + CA_WIDTH

F32 = jnp.float32
BF16 = jnp.bfloat16
NEG_INF = float("-inf")


def _nt_dot(a, b):
    return lax.dot_general(a, b, (((1,), (1,)), ((), ())), preferred_element_type=F32)


def _dot(a, b):
    return jnp.dot(a, b, preferred_element_type=F32)


def _silu(g):
    return g * (1.0 / (1.0 + jnp.exp(-g)))


def _select_rows(q_pair, lo, hi):
    row = lax.broadcasted_iota(jnp.int32, q_pair.shape, 0)
    return jnp.where((row >= lo) & (row < hi), q_pair, jnp.zeros_like(q_pair))


def _head_rows(hd):
    return slice(hd * HEAD_DIM, (hd + 1) * HEAD_DIM)


def _with_ones_rows(v):
    return jnp.concatenate([v, jnp.ones((ONES_ROWS, v.shape[1]), v.dtype)], axis=0)


def _key_tile(k_ref, j):
    return k_ref[0, pl.ds(pl.multiple_of(j * ATTN_TILE, ATTN_TILE), ATTN_TILE), :]


def _causal_key_loop(step, state, tiles, width):
    i0 = Q_TILES * step
    slots = range(Q_TILES)

    def head(r):
        keys = [i0 + 1, i0] + [i0 - 1 - m for m in range(r)]
        work = [(1, 0, True), (1, 1, False), (0, 1, True)] + [(qt, 2 + m, False) for m in range(r) for qt in slots]
        return lambda st: tiles(st, keys, work)

    rems = list(range(0, width, Q_TILES))
    rem = i0 % width
    if len(rems) == 1:
        state = head(0)(state)
    else:
        state = lax.switch(rem // Q_TILES, [head(r) for r in rems], state)
    top = i0 - 1 - rem

    def body(n, st):
        j = top - width * n
        return tiles(st, [j - m for m in range(width)], [(qt, m, False) for m in range(width) for qt in slots])

    return lax.fori_loop(0, i0 // width, body, state)


def _attn_specs(ns, s, t, row_q, col_k, row_v, row_g, q_tiles, resident):
    q = pl.BlockSpec((1, q_tiles, LANES, t), lambda bi, p, i: (bi, i, row_q // LANES + p, 0))
    g = pl.BlockSpec((1, q_tiles, LANES, t), lambda bi, p, i: (bi, i, row_g // LANES + p, 0))
    if resident:
        k = pl.BlockSpec((1, s, LANES), lambda bi, p, i: (bi, 0, col_k // LANES + p))
        v = pl.BlockSpec((1, ns, LANES, t), lambda bi, p, i: (bi, 0, row_v // LANES + p, 0))
        return q, k, v, g
    return q, g


def _in_proj_kernel(x_ref, g_ref, wt_ref, wk_ref, wkct_ref, pos_ref, invf_ref, qvg_ref, k_ref):
    x = x_ref[0]
    ms = jnp.mean(x * x, axis=-1, keepdims=True)
    h = (x * lax.rsqrt(ms + EPS) * g_ref[...]).astype(BF16)
    ts = h.shape[0]

    ang = pos_ref[0].astype(F32) * invf_ref[...]
    cos, sin = jnp.cos(ang), jnp.sin(ang)
    half = DA_QKDIM // 2

    def rotary_t(t):
        parts = []
        for grp in range(DA_WIDTH // DA_QKDIM):
            x1 = t[grp * DA_QKDIM:grp * DA_QKDIM + half]
            x2 = t[grp * DA_QKDIM + half:(grp + 1) * DA_QKDIM]
            parts.append(x1 * cos - x2 * sin)
            parts.append(x2 * cos + x1 * sin)
        return jnp.concatenate(parts, axis=0)

    def store_rows(r0, val):
        for t in range(ts // ATTN_TILE):
            qvg_ref[0, t, r0:r0 + val.shape[0], :] = val[:, t * ATTN_TILE:(t + 1) * ATTN_TILE].astype(BF16)

    chunk = SB_WIDTH
    for r0 in range(0, QVG_ROWS, chunk):
        res = _nt_dot(wt_ref[r0:r0 + chunk, :], h)
        if r0 == ROW_QA:
            store_rows(r0, res * (HEAD_DIM ** -0.5))
        elif r0 == ROW_QB:
            store_rows(r0, res * (HEAD_DIM ** -0.5 * LOG2E))
        elif r0 == ROW_QC:
            store_rows(r0, rotary_t(res[:DA_WIDTH]) * DA_EXP2_SCALE)
            store_rows(r0 + DA_WIDTH, res[DA_WIDTH:])
        else:
            store_rows(r0, res)

    kab = _dot(h, wk_ref[...])
    k_ref[0, :, 0:COL_KC] = kab.astype(BF16)
    kct = rotary_t(_nt_dot(wkct_ref[...], h))
    k_ref[0, :, COL_KC:K_COLS] = kct.T.astype(BF16)


def _in_proj(x, pre_g, wt, wk, wkct, pos3, invf):
    b, s, d = x.shape
    ts = min(PROJ_TILE, s)
    ns = s // ATTN_TILE
    const = lambda *_: (0, 0)
    return pl.pallas_call(
        _in_proj_kernel,
        out_shape=(jax.ShapeDtypeStruct((b, ns, QVG_ROWS, ATTN_TILE), BF16),
                   jax.ShapeDtypeStruct((b, s, K_COLS), BF16)),
        grid=(b, s // ts),
        in_specs=[
            pl.BlockSpec((1, ts, d), lambda bi, si: (bi, si, 0)),
            pl.BlockSpec((1, d), const),
            pl.BlockSpec(wt.shape, const),
            pl.BlockSpec(wk.shape, const),
            pl.BlockSpec(wkct.shape, const),
            pl.BlockSpec((1, 1, ts), lambda bi, si: (bi, 0, si)),
            pl.BlockSpec(invf.shape, const),
        ],
        out_specs=(
            pl.BlockSpec((1, ts // ATTN_TILE, QVG_ROWS, ATTN_TILE), lambda bi, si: (bi, si, 0, 0)),
            pl.BlockSpec((1, ts, K_COLS), lambda bi, si: (bi, si, 0)),
        ),
        compiler_params=pltpu.CompilerParams(
            dimension_semantics=("parallel", "parallel"), vmem_limit_bytes=VMEM_LIMIT),
        name="in_proj",
    )(x, pre_g, wt, wk, wkct, pos3, invf)


def _sb_kernel(q_ref, k_ref, v_ref, g_ref, tri_ref, o_ref):
    t = ATTN_TILE
    heads = range(PAIR)
    qs = [[_select_rows(q_ref[0, qt], hd * HEAD_DIM, (hd + 1) * HEAD_DIM) for hd in heads]
          for qt in range(Q_TILES)]
    neg_tri = tri_ref[...]

    def tiles(state, keys, work):
        kts = [_key_tile(k_ref, j) for j in keys]
        strict = (lax.broadcasted_iota(jnp.int32, (t, t), 0)
                  < lax.broadcasted_iota(jnp.int32, (t, t), 1))
        chains = [(qt, hd, kpos, diag) for qt, kpos, diag in work for hd in heads]
        carries = [[state[qt][hd][0] for hd in heads] for qt in range(Q_TILES)]
        accs = [[state[qt][hd][1] for hd in heads] for qt in range(Q_TILES)]
        n_chains = len(chains)
        scores, tails = [None] * n_chains, [None] * n_chains

        def score(n):
            qt, hd, kpos, diag = chains[n]
            s = _dot(kts[kpos], qs[qt][hd])
            if diag:
                s = jnp.where(strict, s, NEG_INF)
            scores[n] = s

        def tail(n):
            s = scores[n]
            sp = (jnp.maximum(s, 0.0) + jnp.log(1.0 + jnp.exp2(jnp.abs(s) * -LOG2E))).astype(BF16)
            tails[n] = _dot(neg_tri, sp)

        def weigh(n):
            qt, hd, kpos, _ = chains[n]
            w = jnp.exp(scores[n] + tails[n]).astype(BF16)
            carry = carries[qt][hd]
            pv = _dot(v_ref[0, keys[kpos], _head_rows(hd), :], w)
            accs[qt][hd] = accs[qt][hd] + pv * jnp.exp(carry)
            carries[qt][hd] = carry + tails[n][0:1]

        for n in range(n_chains + 2 * CHAIN_LAG):
            if n < n_chains:
                score(n)
            if 0 <= n - CHAIN_LAG < n_chains:
                tail(n - CHAIN_LAG)
            if 0 <= n - 2 * CHAIN_LAG < n_chains:
                weigh(n - 2 * CHAIN_LAG)
        return tuple(tuple((carries[qt][hd], accs[qt][hd]) for hd in heads) for qt in range(Q_TILES))

    init = tuple(tuple((jnp.zeros((1, t), F32), jnp.zeros((HEAD_DIM, t), F32)) for _ in heads)
                 for _ in range(Q_TILES))

    def live(state):
        carry = functools.reduce(jnp.maximum, [state[qt][hd][0] for qt in range(Q_TILES) for hd in heads])
        return jnp.max(carry) > SB_DEAD_CARRY

    step = pl.program_id(2)
    i0 = Q_TILES * step
    both = [(qt, False) for qt in range(Q_TILES)]

    def sweep(keys, slot_diags):
        return lambda st: tiles(st, keys, [(qt, kpos, diag) for kpos, row in enumerate(slot_diags) for qt, diag in row])

    first = sweep([i0 + 1, i0], [[(1, True)], [(1, False), (0, True)]])
    near = sweep([i0 + 1, i0, i0 - 1], [[(1, True)], [(1, False), (0, True)], both])
    state = lax.cond(step == 0, first, near, init)
    top = i0 - 2

    def body(n, st):
        j = top - SB_KEYS_PER_TRIP * n
        return sweep([j - m for m in range(SB_KEYS_PER_TRIP)], [both] * SB_KEYS_PER_TRIP)(st)

    trips = jnp.maximum(i0 - 1, 0) // SB_KEYS_PER_TRIP
    _, state = lax.while_loop(lambda c: (c[0] < trips) & live(c[1]), lambda c: (c[0] + 1, body(c[0], c[1])),
                              (jnp.int32(0), state))
    state = lax.cond((step > 0) & live(state), sweep([i0 * 0], [both]), lambda st: st, state)
    for qt in range(Q_TILES):
        out = jnp.concatenate([acc for _, acc in state[qt]], axis=0)
        o_ref[0, qt] = (out * _silu(g_ref[0, qt].astype(F32))).astype(BF16)


def _sb_attention(qvg, k, neg_tri):
    b, ns, _, t = qvg.shape
    s = k.shape[1]
    return pl.pallas_call(
        _sb_kernel,
        out_shape=jax.ShapeDtypeStruct((b, ns, SB_WIDTH, t), BF16),
        grid=(b, SB_HEADS // PAIR, ns // Q_TILES),
        in_specs=[*_attn_specs(ns, s, t, ROW_QA, COL_KA, ROW_VA, ROW_GA, Q_TILES, True),
                  pl.BlockSpec((t, t), lambda bi, p, i: (0, 0))],
        out_specs=pl.BlockSpec((1, Q_TILES, LANES, t), lambda bi, p, i: (bi, i, p, 0)),
        compiler_params=pltpu.CompilerParams(
            dimension_semantics=("parallel", "parallel", "arbitrary"), vmem_limit_bytes=VMEM_LIMIT),
        name="sb_attn",
    )(qvg, k, qvg, qvg, neg_tri)


CA_BLOCKS = CA_LEFT_CHUNKS * CHUNK // ATTN_TILE + 1


CA_Q_TILES = 4
CA_KEY_BLOCKS = CA_BLOCKS - 1 + CA_Q_TILES


def _ca_kernel(q_ref, *refs):
    k_refs, v_refs = refs[:CA_KEY_BLOCKS], refs[CA_KEY_BLOCKS:2 * CA_KEY_BLOCKS]
    g_ref, bias_ref, o_ref = refs[2 * CA_KEY_BLOCKS:]
    back = CA_BLOCKS - 1
    started = pl.program_id(2) > 0
    groups = [(qt, hd) for qt in range(CA_Q_TILES) for hd in range(PAIR)]
    scores, probs, outs = {}, {}, {}

    def score(qt, hd):
        q = _select_rows(q_ref[0, qt], hd * HEAD_DIM, (hd + 1) * HEAD_DIM)
        scores[qt, hd] = []
        for d in range(CA_BLOCKS):
            s = _dot(k_refs[qt + d][0], q) + bias_ref[hd, d]
            if qt + d < back:
                s = jnp.where(started, s, NEG_INF)
            scores[qt, hd].append(s)

    def normalise(qt, hd):
        ss = scores[qt, hd]
        m = functools.reduce(jnp.maximum, [jnp.max(s, axis=0, keepdims=True) for s in ss])
        probs[qt, hd] = [jnp.exp2(s - m).astype(BF16) for s in ss]

    def accumulate(qt, hd):
        acc = functools.reduce(jnp.add, [_dot(_with_ones_rows(v_refs[qt + d][0, 0, _head_rows(hd), :]),
                                              probs[qt, hd][d]) for d in range(CA_BLOCKS)])
        outs[qt, hd] = acc[:HEAD_DIM] / acc[HEAD_DIM:HEAD_DIM + 1]

    for stage in (score, normalise, accumulate):
        for qt, hd in groups:
            stage(qt, hd)
    for qt in range(CA_Q_TILES):
        out = jnp.concatenate([outs[qt, hd] for hd in range(PAIR)], axis=0)
        o_ref[0, qt] = (out * _silu(g_ref[0, qt].astype(F32))).astype(BF16)


def _ca_attention(qvg, k, bias):
    b, ns, _, t = qvg.shape
    back = CA_BLOCKS - 1
    qspec, gspec = _attn_specs(ns, k.shape[1], t, ROW_QB, COL_KB, ROW_VB, ROW_GB, CA_Q_TILES, False)

    def kspec(n):
        return pl.BlockSpec((1, t, LANES),
                            lambda bi, p, i: (bi, jnp.maximum(CA_Q_TILES * i - back + n, 0), COL_KB // LANES + p))

    def vspec(n):
        return pl.BlockSpec((1, 1, LANES, t),
                            lambda bi, p, i: (bi, jnp.maximum(CA_Q_TILES * i - back + n, 0), ROW_VB // LANES + p, 0))

    return pl.pallas_call(
        _ca_kernel,
        out_shape=jax.ShapeDtypeStruct((b, ns, CA_WIDTH, t), BF16),
        grid=(b, CA_HEADS // PAIR, ns // CA_Q_TILES),
        in_specs=[qspec] + [kspec(n) for n in range(CA_KEY_BLOCKS)] + [vspec(n) for n in range(CA_KEY_BLOCKS)]
        + [gspec, pl.BlockSpec((PAIR, CA_BLOCKS, t, t), lambda bi, p, i: (p, 0, 0, 0))],
        out_specs=pl.BlockSpec((1, CA_Q_TILES, LANES, t), lambda bi, p, i: (bi, i, p, 0)),
        compiler_params=pltpu.CompilerParams(
            dimension_semantics=("parallel", "parallel", "arbitrary"), vmem_limit_bytes=VMEM_LIMIT),
        name="ca_attn",
    )(qvg, *[k] * CA_KEY_BLOCKS, *[qvg] * CA_KEY_BLOCKS, qvg, bias)


def _ca_bias(rel_table):
    t = ATTN_TILE
    h = rel_table.shape[0]
    jk = np.arange(t)[:, None]
    iq = np.arange(t)[None, :]
    delta = np.arange(-(t - 1), t + 1)
    tiles = []
    for d in range(CA_BLOCKS):
        back = (CA_BLOCKS - 1 - d) * t
        idx = np.clip(back + delta, -MAX_REL, MAX_REL) + MAX_REL
        diag_vals = rel_table.astype(F32)[:, idx]
        flat = jnp.tile(diag_vals, (1, t))[:, :t * (2 * t - 1)]
        toeplitz = flat.reshape(h, t, 2 * t - 1)[:, :, t - 1:]
        lag = back // CHUNK + iq // CHUNK - jk // CHUNK
        valid = (lag >= 0) & (lag <= CA_LEFT_CHUNKS)
        tiles.append(jnp.where(valid[None], toeplitz * LOG2E, NEG_INF))
    return jnp.stack(tiles, axis=1)


def _da_kernel(q_ref, k_ref, v_ref, g_ref, lamv_ref, subg_ref, o_ref, *, lam_init):
    t = ATTN_TILE
    maps = range(2 * PAIR)
    qs = [[_select_rows(q_ref[0, qt], c * DA_QKDIM, (c + 1) * DA_QKDIM) for c in maps] for qt in range(Q_TILES)]
    chunk_shift = int(math.log2(CHUNK))

    def tiles(state, keys, work):
        kts = [_key_tile(k_ref, j) for j in keys]
        mask = (lax.shift_right_logical(lax.broadcasted_iota(jnp.int32, (t, t), 0), chunk_shift)
                <= lax.shift_right_logical(lax.broadcasted_iota(jnp.int32, (t, t), 1), chunk_shift))
        groups = [(qt, c) for qt in range(Q_TILES) for c in maps]
        scores, stats, probs, new = {}, {}, {}, {}

        def score(qt, c):
            scores[qt, c] = []
            for wq, kpos, diag in work:
                if wq == qt:
                    s = _dot(kts[kpos], qs[qt][c])
                    if diag:
                        s = jnp.where(mask, s, NEG_INF)
                    scores[qt, c].append((kpos, s))

        def normalise(qt, c):
            m = state[qt][c][0]
            m_new = functools.reduce(jnp.maximum,
                                     [m] + [jnp.max(s, axis=0, keepdims=True) for _, s in scores[qt, c]])
            stats[qt, c] = (m_new, jnp.exp2(m - m_new))
            probs[qt, c] = [(kpos, jnp.exp2(s - m_new).astype(BF16)) for kpos, s in scores[qt, c]]

        def accumulate(qt, c):
            m_new, alpha = stats[qt, c]
            acc = alpha * state[qt][c][1]
            for kpos, p in probs[qt, c]:
                acc = acc + _dot(_with_ones_rows(v_ref[0, keys[kpos], _head_rows(c // 2), :]), p)
            new[qt, c] = (m_new, acc)

        for stage in (score, normalise, accumulate):
            for qt, c in groups:
                stage(qt, c)
        return tuple(tuple(new[qt, c] for c in maps) for qt in range(Q_TILES))

    init = tuple(tuple((jnp.full((1, t), NEG_INF, F32), jnp.zeros((HEAD_DIM + ONES_ROWS, t), F32))
                       for _ in maps) for _ in range(Q_TILES))
    state = _causal_key_loop(pl.program_id(2), init, tiles, DA_KEYS_PER_TRIP)

    lamv = lamv_ref[...]
    lam = (jnp.exp(jnp.sum(lamv[0:1] * lamv[1:2], axis=-1, keepdims=True))
           - jnp.exp(jnp.sum(lamv[2:3] * lamv[3:4], axis=-1, keepdims=True)) + lam_init)
    for qt in range(Q_TILES):
        outs = []
        for hd in range(PAIR):
            (_, acc1), (_, acc2) = state[qt][2 * hd], state[qt][2 * hd + 1]
            o = (acc1[:HEAD_DIM] / acc1[HEAD_DIM:HEAD_DIM + 1]
                 - lam * (acc2[:HEAD_DIM] / acc2[HEAD_DIM:HEAD_DIM + 1]))
            o = o * lax.rsqrt(jnp.mean(o * o, axis=0, keepdims=True) + EPS) * subg_ref[...]
            outs.append(o * (1.0 - lam_init))
        out = jnp.concatenate(outs, axis=0)
        o_ref[0, qt] = (out * _silu(g_ref[0, qt].astype(F32))).astype(BF16)


def _da_attention(qvg, k, lamv, subg, lam_init):
    b, ns, _, t = qvg.shape
    s = k.shape[1]
    return pl.pallas_call(
        functools.partial(_da_kernel, lam_init=lam_init),
        out_shape=jax.ShapeDtypeStruct((b, ns, DA_WIDTH, t), BF16),
        grid=(b, DA_HEADS // PAIR, ns // Q_TILES),
        in_specs=[*_attn_specs(ns, s, t, ROW_QC, COL_KC, ROW_VC, ROW_GC, Q_TILES, True),
                  pl.BlockSpec(lamv.shape, lambda bi, p, i: (0, 0)),
                  pl.BlockSpec(subg.shape, lambda bi, p, i: (0, 0))],
        out_specs=pl.BlockSpec((1, Q_TILES, LANES, t), lambda bi, p, i: (bi, i, p, 0)),
        compiler_params=pltpu.CompilerParams(
            dimension_semantics=("parallel", "parallel", "arbitrary"), vmem_limit_bytes=VMEM_LIMIT),
        name="da_attn",
    )(qvg, k, qvg, qvg, lamv, subg)


def _out_proj_kernel(ya_ref, yb_ref, yc_ref, wo_ref, g_ref, x_ref, o_ref):
    t = ATTN_TILE
    for n in range(ya_ref.shape[1]):
        y_in = jnp.concatenate([ya_ref[0, n], yb_ref[0, n], yc_ref[0, n]], axis=0)
        yt = _dot(wo_ref[...], y_in)
        yt = yt * lax.rsqrt(jnp.mean(yt * yt, axis=0, keepdims=True) + EPS)
        rows = slice(n * t, (n + 1) * t)
        o_ref[0, rows, :] = x_ref[0, rows, :] + yt.T * g_ref[...]


def _out_proj(ya, yb, yc, wot, post_g, x):
    b, s, d = x.shape
    ts = min(PROJ_TILE, s)
    nt = ts // ATTN_TILE
    const = lambda *_: (0, 0)
    yspec = lambda w: pl.BlockSpec((1, nt, w, ATTN_TILE), lambda bi, si: (bi, si, 0, 0))
    return pl.pallas_call(
        _out_proj_kernel,
        out_shape=jax.ShapeDtypeStruct((b, s, d), F32),
        grid=(b, s // ts),
        in_specs=[yspec(SB_WIDTH), yspec(CA_WIDTH), yspec(DA_WIDTH),
                  pl.BlockSpec(wot.shape, const),
                  pl.BlockSpec((1, d), const),
                  pl.BlockSpec((1, ts, d), lambda bi, si: (bi, si, 0))],
        out_specs=pl.BlockSpec((1, ts, d), lambda bi, si: (bi, si, 0)),
        compiler_params=pltpu.CompilerParams(
            dimension_semantics=("parallel", "parallel"), vmem_limit_bytes=VMEM_LIMIT),
        name="out_proj",
    )(ya, yb, yc, wot, post_g, x)


def _split_w_in(w_in):
    a, bb, c = SB_WIDTH, CA_WIDTH, DA_WIDTH
    o_b, o_c = 4 * a, 4 * (a + bb)
    col = lambda o, w, n: w_in[:, o + n * w:o + (n + 1) * w]
    qvg = jnp.concatenate([col(0, a, 0), col(0, a, 2), col(0, a, 3),
                           col(o_b, bb, 0), col(o_b, bb, 2), col(o_b, bb, 3),
                           col(o_c, c, 0), col(o_c, c, 2), col(o_c, c, 3)], axis=1)
    wt = qvg.T.astype(BF16)
    wk = jnp.concatenate([col(0, a, 1), col(o_b, bb, 1)], axis=1).astype(BF16)
    wkct = col(o_c, c, 1).T.astype(BF16)
    return wt, wk, wkct


def _layer(x, pos3, invf, neg_tri, w_in, w_out, pre_g, post_g, rel_table, lamv, subln_g, lam_init):
    wt, wk, wkct = _split_w_in(w_in)
    qvg, k = _in_proj(x, pre_g[None, :], wt, wk, wkct, pos3, invf)
    ya = _sb_attention(qvg, k, neg_tri)
    yb = _ca_attention(qvg, k, _ca_bias(rel_table))
    yc = _da_attention(qvg, k, lamv, subln_g[:, None], lam_init)
    return _out_proj(ya, yb, yc, w_out.T.astype(BF16), post_g[None, :], x)


def kernel(x, positions, w_in, w_out, pre_gain, post_gain, rel_bias, lambda_q1, lambda_k1, lambda_q2,
           lambda_k2, subln_gain):
    b, s, d = x.shape
    assert d == D_MODEL and s % PROJ_TILE == 0
    half = DA_QKDIM // 2
    invf = jnp.asarray((ROPE_THETA ** (-np.arange(half, dtype=np.float32) / half))[:, None])
    pos3 = positions.reshape(b, 1, s)
    t = ATTN_TILE
    neg_tri = jnp.asarray(-(np.arange(t)[None, :] >= np.arange(t)[:, None]).astype(np.float32), dtype=BF16)
    for layer in range(w_in.shape[0]):
        lam_init = 0.8 - 0.6 * math.exp(-0.3 * layer)
        lamv = jnp.stack([lambda_q1[layer], lambda_k1[layer], lambda_q2[layer], lambda_k2[layer]])
        x = _layer(x, pos3, invf, neg_tri, w_in[layer], w_out[layer], pre_gain[layer], post_gain[layer],
                   rel_bias[layer], lamv, subln_gain[layer], lam_init)
    return x
```

```python
import functools
import math

import numpy as np
import jax
import jax.numpy as jnp
from jax import lax
from jax.experimental import pallas as pl
from jax.experimental.pallas import tpu as pltpu

D_MODEL = 1024
CHUNK = 64
HEAD_DIM = 64
SB_HEADS = 6
CA_HEADS = 6
CA_LEFT_CHUNKS = 8
MAX_REL = 256
DA_HEADS = 4
DA_QKDIM = 32
ROPE_THETA = 10000.0
EPS = 1e-6

SB_WIDTH = SB_HEADS * HEAD_DIM
CA_WIDTH = CA_HEADS * HEAD_DIM
DA_WIDTH = DA_HEADS * HEAD_DIM
QVG_ROWS = 3 * (SB_WIDTH + CA_WIDTH + DA_WIDTH)
K_COLS = SB_WIDTH + CA_WIDTH + DA_WIDTH

ATTN_TILE = 256
PROJ_TILE = 512
LANES = 128
PAIR = LANES // HEAD_DIM
ONES_ROWS = 16
Q_TILES = 2
SB_KEYS_PER_TRIP = 2
DA_Q_TILES = 4
DA_KEYS_PER_TRIP = 4
CHAIN_LAG = 2
VMEM_LIMIT = 48 * 1024 * 1024
LOG2E = math.log2(math.e)
SB_DEAD_CARRY = -160.0 / LOG2E
DA_EXP2_SCALE = DA_QKDIM ** -0.5 * LOG2E

ROW_QA, ROW_VA, ROW_GA = 0, SB_WIDTH, 2 * SB_WIDTH
ROW_QB = 3 * SB_WIDTH
ROW_VB, ROW_GB = ROW_QB + CA_WIDTH, ROW_QB + 2 * CA_WIDTH
ROW_QC = ROW_QB + 3 * CA_WIDTH
ROW_VC, ROW_GC = ROW_QC + DA_WIDTH, ROW_QC + 2 * DA_WIDTH
COL_KA, COL_KB, COL_KC = 0, SB_WIDTH, SB_WIDTH + CA_WIDTH

F32 = jnp.float32
BF16 = jnp.bfloat16
NEG_INF = float("-inf")


def _nt_dot(a, b):
    return lax.dot_general(a, b, (((1,), (1,)), ((), ())), preferred_element_type=F32)


def _dot(a, b):
    return jnp.dot(a, b, preferred_element_type=F32)


def _silu(g):
    return g * (1.0 / (1.0 + jnp.exp(-g)))


def _select_rows(q_pair, lo, hi):
    row = lax.broadcasted_iota(jnp.int32, q_pair.shape, 0)
    return jnp.where((row >= lo) & (row < hi), q_pair, jnp.zeros_like(q_pair))


def _head_rows(hd):
    return slice(hd * HEAD_DIM, (hd + 1) * HEAD_DIM)


def _with_ones_rows(v):
    return jnp.concatenate([v, jnp.ones((ONES_ROWS, v.shape[1]), v.dtype)], axis=0)


def _key_tile(k_ref, j):
    return k_ref[0, pl.ds(pl.multiple_of(j * ATTN_TILE, ATTN_TILE), ATTN_TILE), :]


def _causal_key_loop(step, state, tiles, q_tiles, width):
    i0 = q_tiles * step
    slots = range(q_tiles)
    top_slot = q_tiles - 1
    near_keys = [i0 + top_slot - kpos for kpos in range(q_tiles)]
    near_work = [(qt, kpos, qt == top_slot - kpos) for kpos in range(q_tiles) for qt in slots if qt >= top_slot - kpos]

    def head(r):
        keys = near_keys + [i0 - 1 - m for m in range(r)]
        work = near_work + [(qt, q_tiles + m, False) for m in range(r) for qt in slots]
        return lambda st: tiles(st, keys, work)

    rems = list(range(0, width, q_tiles))
    rem = i0 % width
    if len(rems) == 1:
        state = head(0)(state)
    else:
        state = lax.switch(rem // q_tiles, [head(r) for r in rems], state)
    top = i0 - 1 - rem

    def body(n, st):
        j = top - width * n
        return tiles(st, [j - m for m in range(width)], [(qt, m, False) for m in range(width) for qt in slots])

    return lax.fori_loop(0, i0 // width, body, state)


def _attn_specs(ns, s, t, row_q, col_k, row_v, row_g, q_tiles, resident):
    q = pl.BlockSpec((1, q_tiles, LANES, t), lambda bi, p, i: (bi, i, row_q // LANES + p, 0))
    g = pl.BlockSpec((1, q_tiles, LANES, t), lambda bi, p, i: (bi, i, row_g // LANES + p, 0))
    if resident:
        k = pl.BlockSpec((1, s, LANES), lambda bi, p, i: (bi, 0, col_k // LANES + p))
        v = pl.BlockSpec((1, ns, LANES, t), lambda bi, p, i: (bi, 0, row_v // LANES + p, 0))
        return q, k, v, g
    return q, g


def _in_proj_kernel(x_ref, g_ref, wt_ref, wk_ref, wkct_ref, pos_ref, invf_ref, qvg_ref, k_ref):
    x = x_ref[0]
    ms = jnp.mean(x * x, axis=-1, keepdims=True)
    h = (x * lax.rsqrt(ms + EPS) * g_ref[...]).astype(BF16)
    ts = h.shape[0]

    ang = pos_ref[0].astype(F32) * invf_ref[...]
    cos, sin = jnp.cos(ang), jnp.sin(ang)
    half = DA_QKDIM // 2

    def rotary_t(t):
        parts = []
        for grp in range(DA_WIDTH // DA_QKDIM):
            x1 = t[grp * DA_QKDIM:grp * DA_QKDIM + half]
            x2 = t[grp * DA_QKDIM + half:(grp + 1) * DA_QKDIM]
            parts.append(x1 * cos - x2 * sin)
            parts.append(x2 * cos + x1 * sin)
        return jnp.concatenate(parts, axis=0)

    def store_rows(r0, val):
        for t in range(ts // ATTN_TILE):
            qvg_ref[0, t, r0:r0 + val.shape[0], :] = val[:, t * ATTN_TILE:(t + 1) * ATTN_TILE].astype(BF16)

    chunk = SB_WIDTH
    for r0 in range(0, QVG_ROWS, chunk):
        res = _nt_dot(wt_ref[r0:r0 + chunk, :], h)
        if r0 == ROW_QA:
            store_rows(r0, res * (HEAD_DIM ** -0.5))
        elif r0 == ROW_QB:
            store_rows(r0, res * (HEAD_DIM ** -0.5 * LOG2E))
        elif r0 == ROW_QC:
            store_rows(r0, rotary_t(res[:DA_WIDTH]) * DA_EXP2_SCALE)
            store_rows(r0 + DA_WIDTH, res[DA_WIDTH:])
        else:
            store_rows(r0, res)

    kab = _dot(h, wk_ref[...])
    k_ref[0, :, 0:COL_KC] = kab.astype(BF16)
    kct = rotary_t(_nt_dot(wkct_ref[...], h))
    k_ref[0, :, COL_KC:K_COLS] = kct.T.astype(BF16)


def _in_proj(x, pre_g, wt, wk, wkct, pos3, invf):
    b, s, d = x.shape
    ts = min(PROJ_TILE, s)
    ns = s // ATTN_TILE
    const = lambda *_: (0, 0)
    return pl.pallas_call(
        _in_proj_kernel,
        out_shape=(jax.ShapeDtypeStruct((b, ns, QVG_ROWS, ATTN_TILE), BF16),
                   jax.ShapeDtypeStruct((b, s, K_COLS), BF16)),
        grid=(b, s // ts),
        in_specs=[
            pl.BlockSpec((1, ts, d), lambda bi, si: (bi, si, 0)),
            pl.BlockSpec((1, d), const),
            pl.BlockSpec(wt.shape, const),
            pl.BlockSpec(wk.shape, const),
            pl.BlockSpec(wkct.shape, const),
            pl.BlockSpec((1, 1, ts), lambda bi, si: (bi, 0, si)),
            pl.BlockSpec(invf.shape, const),
        ],
        out_specs=(
            pl.BlockSpec((1, ts // ATTN_TILE, QVG_ROWS, ATTN_TILE), lambda bi, si: (bi, si, 0, 0)),
            pl.BlockSpec((1, ts, K_COLS), lambda bi, si: (bi, si, 0)),
        ),
        compiler_params=pltpu.CompilerParams(
            dimension_semantics=("parallel", "parallel"), vmem_limit_bytes=VMEM_LIMIT),
        name="in_proj",
    )(x, pre_g, wt, wk, wkct, pos3, invf)


def _sb_kernel(q_ref, k_ref, v_ref, g_ref, tri_ref, o_ref):
    t = ATTN_TILE
    heads = range(PAIR)
    qs = [[_select_rows(q_ref[0, qt], hd * HEAD_DIM, (hd + 1) * HEAD_DIM) for hd in heads]
          for qt in range(Q_TILES)]
    neg_tri = tri_ref[...]

    def tiles(state, keys, work):
        kts = [_key_tile(k_ref, j) for j in keys]
        strict = (lax.broadcasted_iota(jnp.int32, (t, t), 0)
                  < lax.broadcasted_iota(jnp.int32, (t, t), 1))
        chains = [(qt, hd, kpos, diag) for qt, kpos, diag in work for hd in heads]
        carries = [[state[qt][hd][0] for hd in heads] for qt in range(Q_TILES)]
        accs = [[state[qt][hd][1] for hd in heads] for qt in range(Q_TILES)]
        n_chains = len(chains)
        scores, tails = [None] * n_chains, [None] * n_chains

        def score(n):
            qt, hd, kpos, diag = chains[n]
            s = _dot(kts[kpos], qs[qt][hd])
            if diag:
                s = jnp.where(strict, s, NEG_INF)
            scores[n] = s

        def tail(n):
            s = scores[n]
            sp = (jnp.maximum(s, 0.0) + jnp.log(1.0 + jnp.exp2(jnp.abs(s) * -LOG2E))).astype(BF16)
            tails[n] = _dot(neg_tri, sp)

        def weigh(n):
            qt, hd, kpos, _ = chains[n]
            w = jnp.exp(scores[n] + tails[n]).astype(BF16)
            carry = carries[qt][hd]
            pv = _dot(v_ref[0, keys[kpos], _head_rows(hd), :], w)
            accs[qt][hd] = accs[qt][hd] + pv * jnp.exp(carry)
            carries[qt][hd] = carry + tails[n][0:1]

        for n in range(n_chains + 2 * CHAIN_LAG):
            if n < n_chains:
                score(n)
            if 0 <= n - CHAIN_LAG < n_chains:
                tail(n - CHAIN_LAG)
            if 0 <= n - 2 * CHAIN_LAG < n_chains:
                weigh(n - 2 * CHAIN_LAG)
        return tuple(tuple((carries[qt][hd], accs[qt][hd]) for hd in heads) for qt in range(Q_TILES))

    init = tuple(tuple((jnp.zeros((1, t), F32), jnp.zeros((HEAD_DIM, t), F32)) for _ in heads)
                 for _ in range(Q_TILES))

    def live(state):
        carry = functools.reduce(jnp.maximum, [state[qt][hd][0] for qt in range(Q_TILES) for hd in heads])
        return jnp.max(carry) > SB_DEAD_CARRY

    step = pl.program_id(2)
    i0 = Q_TILES * step
    both = [(qt, False) for qt in range(Q_TILES)]

    def sweep(keys, slot_diags):
        return lambda st: tiles(st, keys, [(qt, kpos, diag) for kpos, row in enumerate(slot_diags) for qt, diag in row])

    first = sweep([i0 + 1, i0], [[(1, True)], [(1, False), (0, True)]])
    near = sweep([i0 + 1, i0, i0 - 1], [[(1, True)], [(1, False), (0, True)], both])
    state = lax.cond(step == 0, first, near, init)
    top = i0 - 2

    def body(n, st):
        j = top - SB_KEYS_PER_TRIP * n
        return sweep([j - m for m in range(SB_KEYS_PER_TRIP)], [both] * SB_KEYS_PER_TRIP)(st)

    trips = jnp.maximum(i0 - 1, 0) // SB_KEYS_PER_TRIP
    _, state = lax.while_loop(lambda c: (c[0] < trips) & live(c[1]), lambda c: (c[0] + 1, body(c[0], c[1])),
                              (jnp.int32(0), state))
    state = lax.cond((step > 0) & live(state), sweep([i0 * 0], [both]), lambda st: st, state)
    for qt in range(Q_TILES):
        out = jnp.concatenate([acc for _, acc in state[qt]], axis=0)
        o_ref[0, qt] = (out * _silu(g_ref[0, qt].astype(F32))).astype(BF16)


def _sb_attention(qvg, k, neg_tri):
    b, ns, _, t = qvg.shape
    s = k.shape[1]
    return pl.pallas_call(
        _sb_kernel,
        out_shape=jax.ShapeDtypeStruct((b, ns, SB_WIDTH, t), BF16),
        grid=(b, SB_HEADS // PAIR, ns // Q_TILES),
        in_specs=[*_attn_specs(ns, s, t, ROW_QA, COL_KA, ROW_VA, ROW_GA, Q_TILES, True),
                  pl.BlockSpec((t, t), lambda bi, p, i: (0, 0))],
        out_specs=pl.BlockSpec((1, Q_TILES, LANES, t), lambda bi, p, i: (bi, i, p, 0)),
        compiler_params=pltpu.CompilerParams(
            dimension_semantics=("parallel", "parallel", "arbitrary"), vmem_limit_bytes=VMEM_LIMIT),
        name="sb_attn",
    )(qvg, k, qvg, qvg, neg_tri)


CA_BLOCKS = CA_LEFT_CHUNKS * CHUNK // ATTN_TILE + 1


CA_Q_TILES = 4
CA_KEY_BLOCKS = CA_BLOCKS - 1 + CA_Q_TILES


def _ca_kernel(q_ref, *refs):
    k_refs, v_refs = refs[:CA_KEY_BLOCKS], refs[CA_KEY_BLOCKS:2 * CA_KEY_BLOCKS]
    g_ref, bias_ref, o_ref = refs[2 * CA_KEY_BLOCKS:]
    back = CA_BLOCKS - 1
    started = pl.program_id(2) > 0
    groups = [(qt, hd) for qt in range(CA_Q_TILES) for hd in range(PAIR)]
    scores, probs, outs = {}, {}, {}

    def score(qt, hd):
        q = _select_rows(q_ref[0, qt], hd * HEAD_DIM, (hd + 1) * HEAD_DIM)
        scores[qt, hd] = []
        for d in range(CA_BLOCKS):
            s = _dot(k_refs[qt + d][0], q) + bias_ref[hd, d]
            if qt + d < back:
                s = jnp.where(started, s, NEG_INF)
            scores[qt, hd].append(s)

    def normalise(qt, hd):
        ss = scores[qt, hd]
        m = functools.reduce(jnp.maximum, [jnp.max(s, axis=0, keepdims=True) for s in ss])
        probs[qt, hd] = [jnp.exp2(s - m).astype(BF16) for s in ss]

    def accumulate(qt, hd):
        acc = functools.reduce(jnp.add, [_dot(_with_ones_rows(v_refs[qt + d][0, 0, _head_rows(hd), :]),
                                              probs[qt, hd][d]) for d in range(CA_BLOCKS)])
        outs[qt, hd] = acc[:HEAD_DIM] / acc[HEAD_DIM:HEAD_DIM + 1]

    for stage in (score, normalise, accumulate):
        for qt, hd in groups:
            stage(qt, hd)
    for qt in range(CA_Q_TILES):
        out = jnp.concatenate([outs[qt, hd] for hd in range(PAIR)], axis=0)
        o_ref[0, qt] = (out * _silu(g_ref[0, qt].astype(F32))).astype(BF16)


def _ca_attention(qvg, k, bias):
    b, ns, _, t = qvg.shape
    back = CA_BLOCKS - 1
    qspec, gspec = _attn_specs(ns, k.shape[1], t, ROW_QB, COL_KB, ROW_VB, ROW_GB, CA_Q_TILES, False)

    def kspec(n):
        return pl.BlockSpec((1, t, LANES),
                            lambda bi, p, i: (bi, jnp.maximum(CA_Q_TILES * i - back + n, 0), COL_KB // LANES + p))

    def vspec(n):
        return pl.BlockSpec((1, 1, LANES, t),
                            lambda bi, p, i: (bi, jnp.maximum(CA_Q_TILES * i - back + n, 0), ROW_VB // LANES + p, 0))

    return pl.pallas_call(
        _ca_kernel,
        out_shape=jax.ShapeDtypeStruct((b, ns, CA_WIDTH, t), BF16),
        grid=(b, CA_HEADS // PAIR, ns // CA_Q_TILES),
        in_specs=[qspec] + [kspec(n) for n in range(CA_KEY_BLOCKS)] + [vspec(n) for n in range(CA_KEY_BLOCKS)]
        + [gspec, pl.BlockSpec((PAIR, CA_BLOCKS, t, t), lambda bi, p, i: (p, 0, 0, 0))],
        out_specs=pl.BlockSpec((1, CA_Q_TILES, LANES, t), lambda bi, p, i: (bi, i, p, 0)),
        compiler_params=pltpu.CompilerParams(
            dimension_semantics=("parallel", "parallel", "arbitrary"), vmem_limit_bytes=VMEM_LIMIT),
        name="ca_attn",
    )(qvg, *[k] * CA_KEY_BLOCKS, *[qvg] * CA_KEY_BLOCKS, qvg, bias)


def _ca_bias(rel_table):
    t = ATTN_TILE
    h = rel_table.shape[0]
    jk = np.arange(t)[:, None]
    iq = np.arange(t)[None, :]
    delta = np.arange(-(t - 1), t + 1)
    tiles = []
    for d in range(CA_BLOCKS):
        back = (CA_BLOCKS - 1 - d) * t
        idx = np.clip(back + delta, -MAX_REL, MAX_REL) + MAX_REL
        diag_vals = rel_table.astype(F32)[:, idx]
        flat = jnp.tile(diag_vals, (1, t))[:, :t * (2 * t - 1)]
        toeplitz = flat.reshape(h, t, 2 * t - 1)[:, :, t - 1:]
        lag = back // CHUNK + iq // CHUNK - jk // CHUNK
        valid = (lag >= 0) & (lag <= CA_LEFT_CHUNKS)
        tiles.append(jnp.where(valid[None], toeplitz * LOG2E, NEG_INF))
    return jnp.stack(tiles, axis=1)


def _da_kernel(q_ref, k_ref, v_ref, g_ref, lamv_ref, subg_ref, o_ref, *, lam_init):
    t = ATTN_TILE
    maps = range(2 * PAIR)
    qs = [[_select_rows(q_ref[0, qt], c * DA_QKDIM, (c + 1) * DA_QKDIM) for c in maps] for qt in range(DA_Q_TILES)]
    chunk_shift = int(math.log2(CHUNK))

    def tiles(state, keys, work):
        kts = [_key_tile(k_ref, j) for j in keys]
        mask = (lax.shift_right_logical(lax.broadcasted_iota(jnp.int32, (t, t), 0), chunk_shift)
                <= lax.shift_right_logical(lax.broadcasted_iota(jnp.int32, (t, t), 1), chunk_shift))
        groups = [(qt, c) for qt in range(DA_Q_TILES) for c in maps]
        scores, stats, probs, new = {}, {}, {}, {}

        def score(qt, c):
            scores[qt, c] = []
            for wq, kpos, diag in work:
                if wq == qt:
                    s = _dot(kts[kpos], qs[qt][c])
                    if diag:
                        s = jnp.where(mask, s, NEG_INF)
                    scores[qt, c].append((kpos, s))

        def normalise(qt, c):
            m = state[qt][c][0]
            m_new = functools.reduce(jnp.maximum,
                                     [m] + [jnp.max(s, axis=0, keepdims=True) for _, s in scores[qt, c]])
            stats[qt, c] = (m_new, jnp.exp2(m - m_new))
            probs[qt, c] = [(kpos, jnp.exp2(s - m_new).astype(BF16)) for kpos, s in scores[qt, c]]

        def accumulate(qt, c):
            m_new, alpha = stats[qt, c]
            acc = alpha * state[qt][c][1]
            for kpos, p in probs[qt, c]:
                acc = acc + _dot(_with_ones_rows(v_ref[0, keys[kpos], _head_rows(c // 2), :]), p)
            new[qt, c] = (m_new, acc)

        for stage in (score, normalise, accumulate):
            for qt, c in groups:
                stage(qt, c)
        return tuple(tuple(new[qt, c] for c in maps) for qt in range(DA_Q_TILES))

    init = tuple(tuple((jnp.full((1, t), NEG_INF, F32), jnp.zeros((HEAD_DIM + ONES_ROWS, t), F32))
                       for _ in maps) for _ in range(DA_Q_TILES))
    state = _causal_key_loop(pl.program_id(2), init, tiles, DA_Q_TILES, DA_KEYS_PER_TRIP)

    lamv = lamv_ref[...]
    lam = (jnp.exp(jnp.sum(lamv[0:1] * lamv[1:2], axis=-1, keepdims=True))
           - jnp.exp(jnp.sum(lamv[2:3] * lamv[3:4], axis=-1, keepdims=True)) + lam_init)
    for qt in range(DA_Q_TILES):
        outs = []
        for hd in range(PAIR):
            (_, acc1), (_, acc2) = state[qt][2 * hd], state[qt][2 * hd + 1]
            o = (acc1[:HEAD_DIM] / acc1[HEAD_DIM:HEAD_DIM + 1]
                 - lam * (acc2[:HEAD_DIM] / acc2[HEAD_DIM:HEAD_DIM + 1]))
            o = o * lax.rsqrt(jnp.mean(o * o, axis=0, keepdims=True) + EPS) * subg_ref[...]
            outs.append(o * (1.0 - lam_init))
        out = jnp.concatenate(outs, axis=0)
        o_ref[0, qt] = (out * _silu(g_ref[0, qt].astype(F32))).astype(BF16)


def _da_attention(qvg, k, lamv, subg, lam_init):
    b, ns, _, t = qvg.shape
    s = k.shape[1]
    return pl.pallas_call(
        functools.partial(_da_kernel, lam_init=lam_init),
        out_shape=jax.ShapeDtypeStruct((b, ns, DA_WIDTH, t), BF16),
        grid=(b, DA_HEADS // PAIR, ns // DA_Q_TILES),
        in_specs=[*_attn_specs(ns, s, t, ROW_QC, COL_KC, ROW_VC, ROW_GC, DA_Q_TILES, True),
                  pl.BlockSpec(lamv.shape, lambda bi, p, i: (0, 0)),
                  pl.BlockSpec(subg.shape, lambda bi, p, i: (0, 0))],
        out_specs=pl.BlockSpec((1, DA_Q_TILES, LANES, t), lambda bi, p, i: (bi, i, p, 0)),
        compiler_params=pltpu.CompilerParams(
            dimension_semantics=("parallel", "parallel", "arbitrary"), vmem_limit_bytes=VMEM_LIMIT),
        name="da_attn",
    )(qvg, k, qvg, qvg, lamv, subg)


def _out_proj_kernel(ya_ref, yb_ref, yc_ref, wo_ref, g_ref, x_ref, o_ref):
    t = ATTN_TILE
    for n in range(ya_ref.shape[1]):
        y_in = jnp.concatenate([ya_ref[0, n], yb_ref[0, n], yc_ref[0, n]], axis=0)
        yt = _dot(wo_ref[...], y_in)
        yt = yt * lax.rsqrt(jnp.mean(yt * yt, axis=0, keepdims=True) + EPS)
        rows = slice(n * t, (n + 1) * t)
        o_ref[0, rows, :] = x_ref[0, rows, :] + yt.T * g_ref[...]


def _out_proj(ya, yb, yc, wot, post_g, x):
    b, s, d = x.shape
    ts = min(PROJ_TILE, s)
    nt = ts // ATTN_TILE
    const = lambda *_: (0, 0)
    yspec = lambda w: pl.BlockSpec((1, nt, w, ATTN_TILE), lambda bi, si: (bi, si, 0, 0))
    return pl.pallas_call(
        _out_proj_kernel,
        out_shape=jax.ShapeDtypeStruct((b, s, d), F32),
        grid=(b, s // ts),
        in_specs=[yspec(SB_WIDTH), yspec(CA_WIDTH), yspec(DA_WIDTH),
                  pl.BlockSpec(wot.shape, const),
                  pl.BlockSpec((1, d), const),
                  pl.BlockSpec((1, ts, d), lambda bi, si: (bi, si, 0))],
        out_specs=pl.BlockSpec((1, ts, d), lambda bi, si: (bi, si, 0)),
        compiler_params=pltpu.CompilerParams(
            dimension_semantics=("parallel", "parallel"), vmem_limit_bytes=VMEM_LIMIT),
        name="out_proj",
    )(ya, yb, yc, wot, post_g, x)


def _split_w_in(w_in):
    a, bb, c = SB_WIDTH, CA_WIDTH, DA_WIDTH
    o_b, o_c = 4 * a, 4 * (a + bb)
    col = lambda o, w, n: w_in[:, o + n * w:o + (n + 1) * w]
    qvg = jnp.concatenate([col(0, a, 0), col(0, a, 2), col(0, a, 3),
                           col(o_b, bb, 0), col(o_b, bb, 2), col(o_b, bb, 3),
                           col(o_c, c, 0), col(o_c, c, 2), col(o_c, c, 3)], axis=1)
    wt = qvg.T.astype(BF16)
    wk = jnp.concatenate([col(0, a, 1), col(o_b, bb, 1)], axis=1).astype(BF16)
    wkct = col(o_c, c, 1).T.astype(BF16)
    return wt, wk, wkct


def _layer(x, pos3, invf, neg_tri, w_in, w_out, pre_g, post_g, rel_table, lamv, subln_g, lam_init):
    wt, wk, wkct = _split_w_in(w_in)
    qvg, k = _in_proj(x, pre_g[None, :], wt, wk, wkct, pos3, invf)
    ya = _sb_attention(qvg, k, neg_tri)
    yb = _ca_attention(qvg, k, _ca_bias(rel_table))
    yc = _da_attention(qvg, k, lamv, subln_g[:, None], lam_init)
    return _out_proj(ya, yb, yc, w_out.T.astype(BF16), post_g[None, :], x)


def kernel(x, positions, w_in, w_out, pre_gain, post_gain, rel_bias, lambda_q1, lambda_k1, lambda_q2,
           lambda_k2, subln_gain):
    b, s, d = x.shape
    assert d == D_MODEL and s % PROJ_TILE == 0
    half = DA_QKDIM // 2
    invf = jnp.asarray((ROPE_THETA ** (-np.arange(half, dtype=np.float32) / half))[:, None])
    pos3 = positions.reshape(b, 1, s)
    t = ATTN_TILE
    neg_tri = jnp.asarray(-(np.arange(t)[None, :] >= np.arange(t)[:, None]).astype(np.float32), dtype=BF16)
    for layer in range(w_in.shape[0]):
        lam_init = 0.8 - 0.6 * math.exp(-0.3 * layer)
        lamv = jnp.stack([lambda_q1[layer], lambda_k1[layer], lambda_q2[layer], lambda_k2[layer]])
        x = _layer(x, pos3, invf, neg_tri, w_in[layer], w_out[layer], pre_gain[layer], post_gain[layer],
                   rel_bias[layer], lamv, subln_gain[layer], lam_init)
    return x
```

```python
import functools
import math

import numpy as np
import jax
import jax.numpy as jnp
from jax import lax
from jax.experimental import pallas as pl
from jax.experimental.pallas import tpu as pltpu

D_MODEL = 1024
CHUNK = 64
HEAD_DIM = 64
SB_HEADS = 6
CA_HEADS = 6
CA_LEFT_CHUNKS = 8
MAX_REL = 256
DA_HEADS = 4
DA_QKDIM = 32
ROPE_THETA = 10000.0
EPS = 1e-6

SB_WIDTH = SB_HEADS * HEAD_DIM
CA_WIDTH = CA_HEADS * HEAD_DIM
DA_WIDTH = DA_HEADS * HEAD_DIM
QVG_ROWS = 3 * (SB_WIDTH + CA_WIDTH + DA_WIDTH)
K_COLS = SB_WIDTH + CA_WIDTH + DA_WIDTH

ATTN_TILE = 256
PROJ_TILE = 512
LANES = 128
PAIR = LANES // HEAD_DIM
ONES_ROWS = 16
Q_TILES = 4
SB_KEYS_PER_TRIP = 2
DA_Q_TILES = 4
DA_KEYS_PER_TRIP = 4
CHAIN_LAG = 2
VMEM_LIMIT = 48 * 1024 * 1024
LOG2E = math.log2(math.e)
SB_DEAD_CARRY = -160.0 / LOG2E
DA_EXP2_SCALE = DA_QKDIM ** -0.5 * LOG2E

ROW_QA, ROW_VA, ROW_GA = 0, SB_WIDTH, 2 * SB_WIDTH
ROW_QB = 3 * SB_WIDTH
ROW_VB, ROW_GB = ROW_QB + CA_WIDTH, ROW_QB + 2 * CA_WIDTH
ROW_QC = ROW_QB + 3 * CA_WIDTH
ROW_VC, ROW_GC = ROW_QC + DA_WIDTH, ROW_QC + 2 * DA_WIDTH
COL_KA, COL_KB, COL_KC = 0, SB_WIDTH, SB_WIDTH + CA_WIDTH

F32 = jnp.float32
BF16 = jnp.bfloat16
NEG_INF = float("-inf")


def _nt_dot(a, b):
    return lax.dot_general(a, b, (((1,), (1,)), ((), ())), preferred_element_type=F32)


def _dot(a, b):
    return jnp.dot(a, b, preferred_element_type=F32)


def _silu(g):
    return g * (1.0 / (1.0 + jnp.exp(-g)))


def _select_rows(q_pair, lo, hi):
    row = lax.broadcasted_iota(jnp.int32, q_pair.shape, 0)
    return jnp.where((row >= lo) & (row < hi), q_pair, jnp.zeros_like(q_pair))


def _head_rows(hd):
    return slice(hd * HEAD_DIM, (hd + 1) * HEAD_DIM)


def _with_ones_rows(v):
    return jnp.concatenate([v, jnp.ones((ONES_ROWS, v.shape[1]), v.dtype)], axis=0)


def _key_tile(k_ref, j):
    return k_ref[0, pl.ds(pl.multiple_of(j * ATTN_TILE, ATTN_TILE), ATTN_TILE), :]


def _causal_key_loop(step, state, tiles, q_tiles, width):
    i0 = q_tiles * step
    slots = range(q_tiles)
    top_slot = q_tiles - 1
    near_keys = [i0 + top_slot - kpos for kpos in range(q_tiles)]
    near_work = [(qt, kpos, qt == top_slot - kpos) for kpos in range(q_tiles) for qt in slots if qt >= top_slot - kpos]

    def head(r):
        keys = near_keys + [i0 - 1 - m for m in range(r)]
        work = near_work + [(qt, q_tiles + m, False) for m in range(r) for qt in slots]
        return lambda st: tiles(st, keys, work)

    rems = list(range(0, width, q_tiles))
    rem = i0 % width
    if len(rems) == 1:
        state = head(0)(state)
    else:
        state = lax.switch(rem // q_tiles, [head(r) for r in rems], state)
    top = i0 - 1 - rem

    def body(n, st):
        j = top - width * n
        return tiles(st, [j - m for m in range(width)], [(qt, m, False) for m in range(width) for qt in slots])

    return lax.fori_loop(0, i0 // width, body, state)


def _attn_specs(ns, s, t, row_q, col_k, row_v, row_g, q_tiles, resident):
    q = pl.BlockSpec((1, q_tiles, LANES, t), lambda bi, p, i: (bi, i, row_q // LANES + p, 0))
    g = pl.BlockSpec((1, q_tiles, LANES, t), lambda bi, p, i: (bi, i, row_g // LANES + p, 0))
    if resident:
        k = pl.BlockSpec((1, s, LANES), lambda bi, p, i: (bi, 0, col_k // LANES + p))
        v = pl.BlockSpec((1, ns, LANES, t), lambda bi, p, i: (bi, 0, row_v // LANES + p, 0))
        return q, k, v, g
    return q, g


def _in_proj_kernel(x_ref, g_ref, wt_ref, wk_ref, wkct_ref, pos_ref, invf_ref, qvg_ref, k_ref):
    x = x_ref[0]
    ms = jnp.mean(x * x, axis=-1, keepdims=True)
    h = (x * lax.rsqrt(ms + EPS) * g_ref[...]).astype(BF16)
    ts = h.shape[0]

    ang = pos_ref[0].astype(F32) * invf_ref[...]
    cos, sin = jnp.cos(ang), jnp.sin(ang)
    half = DA_QKDIM // 2

    def rotary_t(t):
        parts = []
        for grp in range(DA_WIDTH // DA_QKDIM):
            x1 = t[grp * DA_QKDIM:grp * DA_QKDIM + half]
            x2 = t[grp * DA_QKDIM + half:(grp + 1) * DA_QKDIM]
            parts.append(x1 * cos - x2 * sin)
            parts.append(x2 * cos + x1 * sin)
        return jnp.concatenate(parts, axis=0)

    def store_rows(r0, val):
        for t in range(ts // ATTN_TILE):
            qvg_ref[0, t, r0:r0 + val.shape[0], :] = val[:, t * ATTN_TILE:(t + 1) * ATTN_TILE].astype(BF16)

    chunk = SB_WIDTH
    for r0 in range(0, QVG_ROWS, chunk):
        res = _nt_dot(wt_ref[r0:r0 + chunk, :], h)
        if r0 == ROW_QA:
            store_rows(r0, res * (HEAD_DIM ** -0.5))
        elif r0 == ROW_QB:
            store_rows(r0, res * (HEAD_DIM ** -0.5 * LOG2E))
        elif r0 == ROW_QC:
            store_rows(r0, rotary_t(res[:DA_WIDTH]) * DA_EXP2_SCALE)
            store_rows(r0 + DA_WIDTH, res[DA_WIDTH:])
        else:
            store_rows(r0, res)

    kab = _dot(h, wk_ref[...])
    k_ref[0, :, 0:COL_KC] = kab.astype(BF16)
    kct = rotary_t(_nt_dot(wkct_ref[...], h))
    k_ref[0, :, COL_KC:K_COLS] = kct.T.astype(BF16)


def _in_proj(x, pre_g, wt, wk, wkct, pos3, invf):
    b, s, d = x.shape
    ts = min(PROJ_TILE, s)
    ns = s // ATTN_TILE
    const = lambda *_: (0, 0)
    return pl.pallas_call(
        _in_proj_kernel,
        out_shape=(jax.ShapeDtypeStruct((b, ns, QVG_ROWS, ATTN_TILE), BF16),
                   jax.ShapeDtypeStruct((b, s, K_COLS), BF16)),
        grid=(b, s // ts),
        in_specs=[
            pl.BlockSpec((1, ts, d), lambda bi, si: (bi, si, 0)),
            pl.BlockSpec((1, d), const),
            pl.BlockSpec(wt.shape, const),
            pl.BlockSpec(wk.shape, const),
            pl.BlockSpec(wkct.shape, const),
            pl.BlockSpec((1, 1, ts), lambda bi, si: (bi, 0, si)),
            pl.BlockSpec(invf.shape, const),
        ],
        out_specs=(
            pl.BlockSpec((1, ts // ATTN_TILE, QVG_ROWS, ATTN_TILE), lambda bi, si: (bi, si, 0, 0)),
            pl.BlockSpec((1, ts, K_COLS), lambda bi, si: (bi, si, 0)),
        ),
        compiler_params=pltpu.CompilerParams(
            dimension_semantics=("parallel", "parallel"), vmem_limit_bytes=VMEM_LIMIT),
        name="in_proj",
    )(x, pre_g, wt, wk, wkct, pos3, invf)


def _sb_kernel(q_ref, k_ref, v_ref, g_ref, tri_ref, o_ref):
    t = ATTN_TILE
    heads = range(PAIR)
    qs = [[_select_rows(q_ref[0, qt], hd * HEAD_DIM, (hd + 1) * HEAD_DIM) for hd in heads]
          for qt in range(Q_TILES)]
    neg_tri = tri_ref[...]

    def tiles(state, keys, work):
        kts = [_key_tile(k_ref, j) for j in keys]
        strict = (lax.broadcasted_iota(jnp.int32, (t, t), 0)
                  < lax.broadcasted_iota(jnp.int32, (t, t), 1))
        chains = [(qt, hd, kpos, diag) for qt, kpos, diag in work for hd in heads]
        carries = [[state[qt][hd][0] for hd in heads] for qt in range(Q_TILES)]
        accs = [[state[qt][hd][1] for hd in heads] for qt in range(Q_TILES)]
        n_chains = len(chains)
        scores, tails = [None] * n_chains, [None] * n_chains

        def score(n):
            qt, hd, kpos, diag = chains[n]
            s = _dot(kts[kpos], qs[qt][hd])
            if diag:
                s = jnp.where(strict, s, NEG_INF)
            scores[n] = s

        def tail(n):
            s = scores[n]
            sp = (jnp.maximum(s, 0.0) + jnp.log(1.0 + jnp.exp2(jnp.abs(s) * -LOG2E))).astype(BF16)
            tails[n] = _dot(neg_tri, sp)

        def weigh(n):
            qt, hd, kpos, _ = chains[n]
            w = jnp.exp(scores[n] + tails[n]).astype(BF16)
            carry = carries[qt][hd]
            pv = _dot(v_ref[0, keys[kpos], _head_rows(hd), :], w)
            accs[qt][hd] = accs[qt][hd] + pv * jnp.exp(carry)
            carries[qt][hd] = carry + tails[n][0:1]

        for n in range(n_chains + 2 * CHAIN_LAG):
            if n < n_chains:
                score(n)
            if 0 <= n - CHAIN_LAG < n_chains:
                tail(n - CHAIN_LAG)
            if 0 <= n - 2 * CHAIN_LAG < n_chains:
                weigh(n - 2 * CHAIN_LAG)
        return tuple(tuple((carries[qt][hd], accs[qt][hd]) for hd in heads) for qt in range(Q_TILES))

    init = tuple(tuple((jnp.zeros((1, t), F32), jnp.zeros((HEAD_DIM, t), F32)) for _ in heads)
                 for _ in range(Q_TILES))

    def live(state):
        carry = functools.reduce(jnp.maximum, [state[qt][hd][0] for qt in range(Q_TILES) for hd in heads])
        return jnp.max(carry) > SB_DEAD_CARRY

    step = pl.program_id(2)
    i0 = Q_TILES * step
    slots = range(Q_TILES)

    def block(ranges):
        offs = sorted({o for hi, lo in ranges.values() for o in range(lo, hi + 1)}, reverse=True)
        work = [(qt, kpos, o == qt) for kpos, o in enumerate(offs) for qt in sorted(ranges)
                if ranges[qt][1] <= o <= ranges[qt][0]]
        return lambda st: tiles(st, [i0 + o for o in offs], work)

    near0 = block({qt: (qt, max(qt - 1, 0)) for qt in slots})
    near = block({qt: (qt, qt - 1) for qt in slots})
    state = lax.cond(step == 0, near0, near, init)
    align0 = block({qt: (qt - 2, 0) for qt in slots if qt >= 2})
    align = block({qt: (qt - 2, -1) for qt in slots if qt >= 1})
    state = lax.switch(jnp.where(live(state), jnp.where(step == 0, 1, 2), 0), [lambda st: st, align0, align], state)

    def sweep(keys):
        return lambda st: tiles(st, keys, [(qt, kpos, False) for kpos in range(len(keys)) for qt in slots])

    def body(n, st):
        j = i0 - 2 - SB_KEYS_PER_TRIP * n
        return sweep([j - m for m in range(SB_KEYS_PER_TRIP)])(st)

    trips = jnp.maximum(i0 - 1, 0) // SB_KEYS_PER_TRIP
    _, state = lax.while_loop(lambda c: (c[0] < trips) & live(c[1]), lambda c: (c[0] + 1, body(c[0], c[1])),
                              (jnp.int32(0), state))
    state = lax.cond((step > 0) & live(state), sweep([i0 * 0]), lambda st: st, state)
    for qt in range(Q_TILES):
        out = jnp.concatenate([acc for _, acc in state[qt]], axis=0)
        o_ref[0, qt] = (out * _silu(g_ref[0, qt].astype(F32))).astype(BF16)


def _sb_attention(qvg, k, neg_tri):
    b, ns, _, t = qvg.shape
    s = k.shape[1]
    return pl.pallas_call(
        _sb_kernel,
        out_shape=jax.ShapeDtypeStruct((b, ns, SB_WIDTH, t), BF16),
        grid=(b, SB_HEADS // PAIR, ns // Q_TILES),
        in_specs=[*_attn_specs(ns, s, t, ROW_QA, COL_KA, ROW_VA, ROW_GA, Q_TILES, True),
                  pl.BlockSpec((t, t), lambda bi, p, i: (0, 0))],
        out_specs=pl.BlockSpec((1, Q_TILES, LANES, t), lambda bi, p, i: (bi, i, p, 0)),
        compiler_params=pltpu.CompilerParams(
            dimension_semantics=("parallel", "parallel", "arbitrary"), vmem_limit_bytes=VMEM_LIMIT),
        name="sb_attn",
    )(qvg, k, qvg, qvg, neg_tri)


CA_BLOCKS = CA_LEFT_CHUNKS * CHUNK // ATTN_TILE + 1


CA_Q_TILES = 4
CA_KEY_BLOCKS = CA_BLOCKS - 1 + CA_Q_TILES


def _ca_kernel(q_ref, *refs):
    k_refs, v_refs = refs[:CA_KEY_BLOCKS], refs[CA_KEY_BLOCKS:2 * CA_KEY_BLOCKS]
    g_ref, bias_ref, o_ref = refs[2 * CA_KEY_BLOCKS:]
    back = CA_BLOCKS - 1
    started = pl.program_id(2) > 0
    groups = [(qt, hd) for qt in range(CA_Q_TILES) for hd in range(PAIR)]
    scores, probs, outs = {}, {}, {}

    def score(qt, hd):
        q = _select_rows(q_ref[0, qt], hd * HEAD_DIM, (hd + 1) * HEAD_DIM)
        scores[qt, hd] = []
        for d in range(CA_BLOCKS):
            s = _dot(k_refs[qt + d][0], q) + bias_ref[hd, d]
            if qt + d < back:
                s = jnp.where(started, s, NEG_INF)
            scores[qt, hd].append(s)

    def normalise(qt, hd):
        ss = scores[qt, hd]
        m = functools.reduce(jnp.maximum, [jnp.max(s, axis=0, keepdims=True) for s in ss])
        probs[qt, hd] = [jnp.exp2(s - m).astype(BF16) for s in ss]

    def accumulate(qt, hd):
        acc = functools.reduce(jnp.add, [_dot(_with_ones_rows(v_refs[qt + d][0, 0, _head_rows(hd), :]),
                                              probs[qt, hd][d]) for d in range(CA_BLOCKS)])
        outs[qt, hd] = acc[:HEAD_DIM] / acc[HEAD_DIM:HEAD_DIM + 1]

    for stage in (score, normalise, accumulate):
        for qt, hd in groups:
            stage(qt, hd)
    for qt in range(CA_Q_TILES):
        out = jnp.concatenate([outs[qt, hd] for hd in range(PAIR)], axis=0)
        o_ref[0, qt] = (out * _silu(g_ref[0, qt].astype(F32))).astype(BF16)


def _ca_attention(qvg, k, bias):
    b, ns, _, t = qvg.shape
    back = CA_BLOCKS - 1
    qspec, gspec = _attn_specs(ns, k.shape[1], t, ROW_QB, COL_KB, ROW_VB, ROW_GB, CA_Q_TILES, False)

    def kspec(n):
        return pl.BlockSpec((1, t, LANES),
                            lambda bi, p, i: (bi, jnp.maximum(CA_Q_TILES * i - back + n, 0), COL_KB // LANES + p))

    def vspec(n):
        return pl.BlockSpec((1, 1, LANES, t),
                            lambda bi, p, i: (bi, jnp.maximum(CA_Q_TILES * i - back + n, 0), ROW_VB // LANES + p, 0))

    return pl.pallas_call(
        _ca_kernel,
        out_shape=jax.ShapeDtypeStruct((b, ns, CA_WIDTH, t), BF16),
        grid=(b, CA_HEADS // PAIR, ns // CA_Q_TILES),
        in_specs=[qspec] + [kspec(n) for n in range(CA_KEY_BLOCKS)] + [vspec(n) for n in range(CA_KEY_BLOCKS)]
        + [gspec, pl.BlockSpec((PAIR, CA_BLOCKS, t, t), lambda bi, p, i: (p, 0, 0, 0))],
        out_specs=pl.BlockSpec((1, CA_Q_TILES, LANES, t), lambda bi, p, i: (bi, i, p, 0)),
        compiler_params=pltpu.CompilerParams(
            dimension_semantics=("parallel", "parallel", "arbitrary"), vmem_limit_bytes=VMEM_LIMIT),
        name="ca_attn",
    )(qvg, *[k] * CA_KEY_BLOCKS, *[qvg] * CA_KEY_BLOCKS, qvg, bias)


def _ca_bias(rel_table):
    t = ATTN_TILE
    h = rel_table.shape[0]
    jk = np.arange(t)[:, None]
    iq = np.arange(t)[None, :]
    delta = np.arange(-(t - 1), t + 1)
    tiles = []
    for d in range(CA_BLOCKS):
        back = (CA_BLOCKS - 1 - d) * t
        idx = np.clip(back + delta, -MAX_REL, MAX_REL) + MAX_REL
        diag_vals = rel_table.astype(F32)[:, idx]
        flat = jnp.tile(diag_vals, (1, t))[:, :t * (2 * t - 1)]
        toeplitz = flat.reshape(h, t, 2 * t - 1)[:, :, t - 1:]
        lag = back // CHUNK + iq // CHUNK - jk // CHUNK
        valid = (lag >= 0) & (lag <= CA_LEFT_CHUNKS)
        tiles.append(jnp.where(valid[None], toeplitz * LOG2E, NEG_INF))
    return jnp.stack(tiles, axis=1)


def _da_kernel(q_ref, k_ref, v_ref, g_ref, lamv_ref, subg_ref, o_ref, *, lam_init):
    t = ATTN_TILE
    maps = range(2 * PAIR)
    qs = [[_select_rows(q_ref[0, qt], c * DA_QKDIM, (c + 1) * DA_QKDIM) for c in maps] for qt in range(DA_Q_TILES)]
    chunk_shift = int(math.log2(CHUNK))

    def tiles(state, keys, work):
        kts = [_key_tile(k_ref, j) for j in keys]
        mask = (lax.shift_right_logical(lax.broadcasted_iota(jnp.int32, (t, t), 0), chunk_shift)
                <= lax.shift_right_logical(lax.broadcasted_iota(jnp.int32, (t, t), 1), chunk_shift))
        groups = [(qt, c) for qt in range(DA_Q_TILES) for c in maps]
        scores, stats, probs, new = {}, {}, {}, {}

        def score(qt, c):
            scores[qt, c] = []
            for wq, kpos, diag in work:
                if wq == qt:
                    s = _dot(kts[kpos], qs[qt][c])
                    if diag:
                        s = jnp.where(mask, s, NEG_INF)
                    scores[qt, c].append((kpos, s))

        def normalise(qt, c):
            m = state[qt][c][0]
            m_new = functools.reduce(jnp.maximum,
                                     [m] + [jnp.max(s, axis=0, keepdims=True) for _, s in scores[qt, c]])
            stats[qt, c] = (m_new, jnp.exp2(m - m_new))
            probs[qt, c] = [(kpos, jnp.exp2(s - m_new).astype(BF16)) for kpos, s in scores[qt, c]]

        def accumulate(qt, c):
            m_new, alpha = stats[qt, c]
            acc = alpha * state[qt][c][1]
            for kpos, p in probs[qt, c]:
                acc = acc + _dot(_with_ones_rows(v_ref[0, keys[kpos], _head_rows(c // 2), :]), p)
            new[qt, c] = (m_new, acc)

        for stage in (score, normalise, accumulate):
            for qt, c in groups:
                stage(qt, c)
        return tuple(tuple(new[qt, c] for c in maps) for qt in range(DA_Q_TILES))

    init = tuple(tuple((jnp.full((1, t), NEG_INF, F32), jnp.zeros((HEAD_DIM + ONES_ROWS, t), F32))
                       for _ in maps) for _ in range(DA_Q_TILES))
    state = _causal_key_loop(pl.program_id(2), init, tiles, DA_Q_TILES, DA_KEYS_PER_TRIP)

    lamv = lamv_ref[...]
    lam = (jnp.exp(jnp.sum(lamv[0:1] * lamv[1:2], axis=-1, keepdims=True))
           - jnp.exp(jnp.sum(lamv[2:3] * lamv[3:4], axis=-1, keepdims=True)) + lam_init)
    for qt in range(DA_Q_TILES):
        outs = []
        for hd in range(PAIR):
            (_, acc1), (_, acc2) = state[qt][2 * hd], state[qt][2 * hd + 1]
            o = (acc1[:HEAD_DIM] / acc1[HEAD_DIM:HEAD_DIM + 1]
                 - lam * (acc2[:HEAD_DIM] / acc2[HEAD_DIM:HEAD_DIM + 1]))
            o = o * lax.rsqrt(jnp.mean(o * o, axis=0, keepdims=True) + EPS) * subg_ref[...]
            outs.append(o * (1.0 - lam_init))
        out = jnp.concatenate(outs, axis=0)
        o_ref[0, qt] = (out * _silu(g_ref[0, qt].astype(F32))).astype(BF16)


def _da_attention(qvg, k, lamv, subg, lam_init):
    b, ns, _, t = qvg.shape
    s = k.shape[1]
    return pl.pallas_call(
        functools.partial(_da_kernel, lam_init=lam_init),
        out_shape=jax.ShapeDtypeStruct((b, ns, DA_WIDTH, t), BF16),
        grid=(b, DA_HEADS // PAIR, ns // DA_Q_TILES),
        in_specs=[*_attn_specs(ns, s, t, ROW_QC, COL_KC, ROW_VC, ROW_GC, DA_Q_TILES, True),
                  pl.BlockSpec(lamv.shape, lambda bi, p, i: (0, 0)),
                  pl.BlockSpec(subg.shape, lambda bi, p, i: (0, 0))],
        out_specs=pl.BlockSpec((1, DA_Q_TILES, LANES, t), lambda bi, p, i: (bi, i, p, 0)),
        compiler_params=pltpu.CompilerParams(
            dimension_semantics=("parallel", "parallel", "arbitrary"), vmem_limit_bytes=VMEM_LIMIT),
        name="da_attn",
    )(qvg, k, qvg, qvg, lamv, subg)


def _out_proj_kernel(ya_ref, yb_ref, yc_ref, wo_ref, g_ref, x_ref, o_ref):
    t = ATTN_TILE
    for n in range(ya_ref.shape[1]):
        y_in = jnp.concatenate([ya_ref[0, n], yb_ref[0, n], yc_ref[0, n]], axis=0)
        yt = _dot(wo_ref[...], y_in)
        yt = yt * lax.rsqrt(jnp.mean(yt * yt, axis=0, keepdims=True) + EPS)
        rows = slice(n * t, (n + 1) * t)
        o_ref[0, rows, :] = x_ref[0, rows, :] + yt.T * g_ref[...]


def _out_proj(ya, yb, yc, wot, post_g, x):
    b, s, d = x.shape
    ts = min(PROJ_TILE, s)
    nt = ts // ATTN_TILE
    const = lambda *_: (0, 0)
    yspec = lambda w: pl.BlockSpec((1, nt, w, ATTN_TILE), lambda bi, si: (bi, si, 0, 0))
    return pl.pallas_call(
        _out_proj_kernel,
        out_shape=jax.ShapeDtypeStruct((b, s, d), F32),
        grid=(b, s // ts),
        in_specs=[yspec(SB_WIDTH), yspec(CA_WIDTH), yspec(DA_WIDTH),
                  pl.BlockSpec(wot.shape, const),
                  pl.BlockSpec((1, d), const),
                  pl.BlockSpec((1, ts, d), lambda bi, si: (bi, si, 0))],
        out_specs=pl.BlockSpec((1, ts, d), lambda bi, si: (bi, si, 0)),
        compiler_params=pltpu.CompilerParams(
            dimension_semantics=("parallel", "parallel"), vmem_limit_bytes=VMEM_LIMIT),
        name="out_proj",
    )(ya, yb, yc, wot, post_g, x)


def _split_w_in(w_in):
    a, bb, c = SB_WIDTH, CA_WIDTH, DA_WIDTH
    o_b, o_c = 4 * a, 4 * (a + bb)
    col = lambda o, w, n: w_in[:, o + n * w:o + (n + 1) * w]
    qvg = jnp.concatenate([col(0, a, 0), col(0, a, 2), col(0, a, 3),
                           col(o_b, bb, 0), col(o_b, bb, 2), col(o_b, bb, 3),
                           col(o_c, c, 0), col(o_c, c, 2), col(o_c, c, 3)], axis=1)
    wt = qvg.T.astype(BF16)
    wk = jnp.concatenate([col(0, a, 1), col(o_b, bb, 1)], axis=1).astype(BF16)
    wkct = col(o_c, c, 1).T.astype(BF16)
    return wt, wk, wkct


def _layer(x, pos3, invf, neg_tri, w_in, w_out, pre_g, post_g, rel_table, lamv, subln_g, lam_init):
    wt, wk, wkct = _split_w_in(w_in)
    qvg, k = _in_proj(x, pre_g[None, :], wt, wk, wkct, pos3, invf)
    ya = _sb_attention(qvg, k, neg_tri)
    yb = _ca_attention(qvg, k, _ca_bias(rel_table))
    yc = _da_attention(qvg, k, lamv, subln_g[:, None], lam_init)
    return _out_proj(ya, yb, yc, w_out.T.astype(BF16), post_g[None, :], x)


def kernel(x, positions, w_in, w_out, pre_gain, post_gain, rel_bias, lambda_q1, lambda_k1, lambda_q2,
           lambda_k2, subln_gain):
    b, s, d = x.shape
    assert d == D_MODEL and s % PROJ_TILE == 0
    half = DA_QKDIM // 2
    invf = jnp.asarray((ROPE_THETA ** (-np.arange(half, dtype=np.float32) / half))[:, None])
    pos3 = positions.reshape(b, 1, s)
    t = ATTN_TILE
    neg_tri = jnp.asarray(-(np.arange(t)[None, :] >= np.arange(t)[:, None]).astype(np.float32), dtype=BF16)
    for layer in range(w_in.shape[0]):
        lam_init = 0.8 - 0.6 * math.exp(-0.3 * layer)
        lamv = jnp.stack([lambda_q1[layer], lambda_k1[layer], lambda_q2[layer], lambda_k2[layer]])
        x = _layer(x, pos3, invf, neg_tri, w_in[layer], w_out[layer], pre_gain[layer], post_gain[layer],
                   rel_bias[layer], lamv, subln_gain[layer], lam_init)
    return x
```

```python
import functools
import math

import numpy as np
import jax
import jax.numpy as jnp
from jax import lax
from jax.experimental import pallas as pl
from jax.experimental.pallas import tpu as pltpu

D_MODEL = 1024
CHUNK = 64
HEAD_DIM = 64
SB_HEADS = 6
CA_HEADS = 6
CA_LEFT_CHUNKS = 8
MAX_REL = 256
DA_HEADS = 4
DA_QKDIM = 32
ROPE_THETA = 10000.0
EPS = 1e-6

SB_WIDTH = SB_HEADS * HEAD_DIM
CA_WIDTH = CA_HEADS * HEAD_DIM
DA_WIDTH = DA_HEADS * HEAD_DIM
QVG_ROWS = 3 * (SB_WIDTH + CA_WIDTH + DA_WIDTH)
K_COLS = SB_WIDTH + CA_WIDTH + DA_WIDTH

ATTN_TILE = 256
PROJ_TILE = 512
LANES = 128
PAIR = LANES // HEAD_DIM
ONES_ROWS = 16
Q_TILES = 4
SB_KEYS_PER_TRIP = 2
DA_Q_TILES = 4
DA_KEYS_PER_TRIP = 4
CHAIN_LAG = 2
VMEM_LIMIT = 48 * 1024 * 1024
LOG2E = math.log2(math.e)
SB_DEAD_CARRY = -160.0 / LOG2E
DA_EXP2_SCALE = DA_QKDIM ** -0.5 * LOG2E

ROW_QA, ROW_VA, ROW_GA = 0, SB_WIDTH, 2 * SB_WIDTH
ROW_QB = 3 * SB_WIDTH
ROW_VB, ROW_GB = ROW_QB + CA_WIDTH, ROW_QB + 2 * CA_WIDTH
ROW_QC = ROW_QB + 3 * CA_WIDTH
ROW_VC, ROW_GC = ROW_QC + DA_WIDTH, ROW_QC + 2 * DA_WIDTH
COL_KA, COL_KB, COL_KC = 0, SB_WIDTH, SB_WIDTH + CA_WIDTH

F32 = jnp.float32
BF16 = jnp.bfloat16
NEG_INF = float("-inf")


def _nt_dot(a, b):
    return lax.dot_general(a, b, (((1,), (1,)), ((), ())), preferred_element_type=F32)


def _dot(a, b):
    return jnp.dot(a, b, preferred_element_type=F32)


def _silu(g):
    return g * (1.0 / (1.0 + jnp.exp(-g)))


def _select_rows(q_pair, lo, hi):
    row = lax.broadcasted_iota(jnp.int32, q_pair.shape, 0)
    return jnp.where((row >= lo) & (row < hi), q_pair, jnp.zeros_like(q_pair))


def _head_rows(hd):
    return slice(hd * HEAD_DIM, (hd + 1) * HEAD_DIM)


def _with_ones_rows(v):
    return jnp.concatenate([v, jnp.ones((ONES_ROWS, v.shape[1]), v.dtype)], axis=0)


def _key_tile(k_ref, j):
    return k_ref[0, pl.ds(pl.multiple_of(j * ATTN_TILE, ATTN_TILE), ATTN_TILE), :]


def _causal_key_loop(step, state, tiles, q_tiles, width):
    i0 = q_tiles * step
    slots = range(q_tiles)
    top_slot = q_tiles - 1
    near_keys = [i0 + top_slot - kpos for kpos in range(q_tiles)]
    near_work = [(qt, kpos, qt == top_slot - kpos) for kpos in range(q_tiles) for qt in slots if qt >= top_slot - kpos]

    def head(r):
        keys = near_keys + [i0 - 1 - m for m in range(r)]
        work = near_work + [(qt, q_tiles + m, False) for m in range(r) for qt in slots]
        return lambda st: tiles(st, keys, work)

    rems = list(range(0, width, q_tiles))
    rem = i0 % width
    if len(rems) == 1:
        state = head(0)(state)
    else:
        state = lax.switch(rem // q_tiles, [head(r) for r in rems], state)
    top = i0 - 1 - rem

    def body(n, st):
        j = top - width * n
        return tiles(st, [j - m for m in range(width)], [(qt, m, False) for m in range(width) for qt in slots])

    return lax.fori_loop(0, i0 // width, body, state)


def _attn_specs(ns, s, t, row_q, col_k, row_v, row_g, q_tiles, resident):
    q = pl.BlockSpec((1, q_tiles, LANES, t), lambda bi, p, i: (bi, i, row_q // LANES + p, 0))
    g = pl.BlockSpec((1, q_tiles, LANES, t), lambda bi, p, i: (bi, i, row_g // LANES + p, 0))
    if resident:
        k = pl.BlockSpec((1, s, LANES), lambda bi, p, i: (bi, 0, col_k // LANES + p))
        v = pl.BlockSpec((1, ns, LANES, t), lambda bi, p, i: (bi, 0, row_v // LANES + p, 0))
        return q, k, v, g
    return q, g


def _in_proj_kernel(x_ref, g_ref, wt_ref, wk_ref, wkct_ref, pos_ref, invf_ref, qvg_ref, k_ref):
    x = x_ref[0]
    ms = jnp.mean(x * x, axis=-1, keepdims=True)
    h = (x * lax.rsqrt(ms + EPS) * g_ref[...]).astype(BF16)
    ts = h.shape[0]

    ang = pos_ref[0].astype(F32) * invf_ref[...]
    cos, sin = jnp.cos(ang), jnp.sin(ang)
    half = DA_QKDIM // 2

    def rotary_t(t):
        parts = []
        for grp in range(DA_WIDTH // DA_QKDIM):
            x1 = t[grp * DA_QKDIM:grp * DA_QKDIM + half]
            x2 = t[grp * DA_QKDIM + half:(grp + 1) * DA_QKDIM]
            parts.append(x1 * cos - x2 * sin)
            parts.append(x2 * cos + x1 * sin)
        return jnp.concatenate(parts, axis=0)

    def store_rows(r0, val):
        for t in range(ts // ATTN_TILE):
            qvg_ref[0, t, r0:r0 + val.shape[0], :] = val[:, t * ATTN_TILE:(t + 1) * ATTN_TILE].astype(BF16)

    chunk = SB_WIDTH
    for r0 in range(0, QVG_ROWS, chunk):
        res = _nt_dot(wt_ref[r0:r0 + chunk, :], h)
        if r0 == ROW_QA:
            store_rows(r0, res * (HEAD_DIM ** -0.5))
        elif r0 == ROW_QB:
            store_rows(r0, res * (HEAD_DIM ** -0.5 * LOG2E))
        elif r0 == ROW_QC:
            store_rows(r0, rotary_t(res[:DA_WIDTH]) * DA_EXP2_SCALE)
            store_rows(r0 + DA_WIDTH, res[DA_WIDTH:])
        else:
            store_rows(r0, res)

    kab = _dot(h, wk_ref[...])
    k_ref[0, :, 0:COL_KC] = kab.astype(BF16)
    kct = rotary_t(_nt_dot(wkct_ref[...], h))
    k_ref[0, :, COL_KC:K_COLS] = kct.T.astype(BF16)


def _in_proj(x, pre_g, wt, wk, wkct, pos3, invf):
    b, s, d = x.shape
    ts = min(PROJ_TILE, s)
    ns = s // ATTN_TILE
    const = lambda *_: (0, 0)
    return pl.pallas_call(
        _in_proj_kernel,
        out_shape=(jax.ShapeDtypeStruct((b, ns, QVG_ROWS, ATTN_TILE), BF16),
                   jax.ShapeDtypeStruct((b, s, K_COLS), BF16)),
        grid=(b, s // ts),
        in_specs=[
            pl.BlockSpec((1, ts, d), lambda bi, si: (bi, si, 0)),
            pl.BlockSpec((1, d), const),
            pl.BlockSpec(wt.shape, const),
            pl.BlockSpec(wk.shape, const),
            pl.BlockSpec(wkct.shape, const),
            pl.BlockSpec((1, 1, ts), lambda bi, si: (bi, 0, si)),
            pl.BlockSpec(invf.shape, const),
        ],
        out_specs=(
            pl.BlockSpec((1, ts // ATTN_TILE, QVG_ROWS, ATTN_TILE), lambda bi, si: (bi, si, 0, 0)),
            pl.BlockSpec((1, ts, K_COLS), lambda bi, si: (bi, si, 0)),
        ),
        compiler_params=pltpu.CompilerParams(
            dimension_semantics=("parallel", "parallel"), vmem_limit_bytes=VMEM_LIMIT),
        name="in_proj",
    )(x, pre_g, wt, wk, wkct, pos3, invf)


def _sb_kernel(q_ref, k_ref, v_ref, g_ref, tri_ref, o_ref):
    t = ATTN_TILE
    heads = range(PAIR)
    qs = [[_select_rows(q_ref[0, qt], hd * HEAD_DIM, (hd + 1) * HEAD_DIM) for hd in heads]
          for qt in range(Q_TILES)]
    neg_tri = tri_ref[...]

    def tiles(state, keys, work):
        kts = [_key_tile(k_ref, j) for j in keys]
        strict = (lax.broadcasted_iota(jnp.int32, (t, t), 0)
                  < lax.broadcasted_iota(jnp.int32, (t, t), 1))
        chains = [(qt, hd, kpos, diag) for qt, kpos, diag in work for hd in heads]
        carries = [[state[qt][hd][0] for hd in heads] for qt in range(Q_TILES)]
        accs = [[state[qt][hd][1] for hd in heads] for qt in range(Q_TILES)]
        n_chains = len(chains)
        scores, tails = [None] * n_chains, [None] * n_chains

        def score(n):
            qt, hd, kpos, diag = chains[n]
            s = _dot(kts[kpos], qs[qt][hd])
            if diag:
                s = jnp.where(strict, s, NEG_INF)
            scores[n] = s

        def tail(n):
            s = scores[n]
            sp = (jnp.maximum(s, 0.0) + jnp.log(1.0 + jnp.exp2(jnp.abs(s) * -LOG2E))).astype(BF16)
            tails[n] = _dot(neg_tri, sp)

        def weigh(n):
            qt, hd, kpos, _ = chains[n]
            w = jnp.exp(scores[n] + tails[n]).astype(BF16)
            carry = carries[qt][hd]
            pv = _dot(v_ref[0, keys[kpos], _head_rows(hd), :], w)
            accs[qt][hd] = accs[qt][hd] + pv * jnp.exp(carry)
            carries[qt][hd] = carry + tails[n][0:1]

        for n in range(n_chains + 2 * CHAIN_LAG):
            if n < n_chains:
                score(n)
            if 0 <= n - CHAIN_LAG < n_chains:
                tail(n - CHAIN_LAG)
            if 0 <= n - 2 * CHAIN_LAG < n_chains:
                weigh(n - 2 * CHAIN_LAG)
        return tuple(tuple((carries[qt][hd], accs[qt][hd]) for hd in heads) for qt in range(Q_TILES))

    init = tuple(tuple((jnp.zeros((1, t), F32), jnp.zeros((HEAD_DIM, t), F32)) for _ in heads)
                 for _ in range(Q_TILES))

    def live(state):
        carry = functools.reduce(jnp.maximum, [state[qt][hd][0] for qt in range(Q_TILES) for hd in heads])
        return jnp.max(carry) > SB_DEAD_CARRY

    step = pl.program_id(2)
    i0 = Q_TILES * step
    slots = range(Q_TILES)

    def block(ranges):
        offs = sorted({o for hi, lo in ranges.values() for o in range(lo, hi + 1)}, reverse=True)
        work = [(qt, kpos, o == qt) for kpos, o in enumerate(offs) for qt in sorted(ranges)
                if ranges[qt][1] <= o <= ranges[qt][0]]
        return lambda st: tiles(st, [i0 + o for o in offs], work)

    near0 = block({qt: (qt, max(qt - 1, 0)) for qt in slots})
    near = block({qt: (qt, qt - 1) for qt in slots})
    state = lax.cond(step == 0, near0, near, init)
    align0 = block({qt: (qt - 2, 0) for qt in slots if qt >= 2})
    align = block({qt: (qt - 2, -1) for qt in slots if qt >= 1})
    state = lax.switch(jnp.where(live(state), jnp.where(step == 0, 1, 2), 0), [lambda st: st, align0, align], state)

    def sweep(keys):
        return lambda st: tiles(st, keys, [(qt, kpos, False) for kpos in range(len(keys)) for qt in slots])

    def body(n, st):
        j = i0 - 2 - SB_KEYS_PER_TRIP * n
        return sweep([j - m for m in range(SB_KEYS_PER_TRIP)])(st)

    trips = jnp.maximum(i0 - 1, 0) // SB_KEYS_PER_TRIP
    _, state = lax.while_loop(lambda c: (c[0] < trips) & live(c[1]), lambda c: (c[0] + 1, body(c[0], c[1])),
                              (jnp.int32(0), state))
    state = lax.cond((step > 0) & live(state), sweep([i0 * 0]), lambda st: st, state)
    for qt in range(Q_TILES):
        out = jnp.concatenate([acc for _, acc in state[qt]], axis=0)
        o_ref[0, qt] = (out * _silu(g_ref[0, qt].astype(F32))).astype(BF16)


def _sb_attention(qvg, k, neg_tri):
    b, ns, _, t = qvg.shape
    s = k.shape[1]
    return pl.pallas_call(
        _sb_kernel,
        out_shape=jax.ShapeDtypeStruct((b, ns, SB_WIDTH, t), BF16),
        grid=(b, SB_HEADS // PAIR, ns // Q_TILES),
        in_specs=[*_attn_specs(ns, s, t, ROW_QA, COL_KA, ROW_VA, ROW_GA, Q_TILES, True),
                  pl.BlockSpec((t, t), lambda bi, p, i: (0, 0))],
        out_specs=pl.BlockSpec((1, Q_TILES, LANES, t), lambda bi, p, i: (bi, i, p, 0)),
        compiler_params=pltpu.CompilerParams(
            dimension_semantics=("parallel", "parallel", "arbitrary"), vmem_limit_bytes=VMEM_LIMIT),
        name="sb_attn",
    )(qvg, k, qvg, qvg, neg_tri)


CA_BLOCKS = CA_LEFT_CHUNKS * CHUNK // ATTN_TILE + 1


CA_Q_TILES = 4
CA_KEY_BLOCKS = CA_BLOCKS - 1 + CA_Q_TILES


def _ca_kernel(q_ref, *refs):
    k_refs, v_refs = refs[:CA_KEY_BLOCKS], refs[CA_KEY_BLOCKS:2 * CA_KEY_BLOCKS]
    g_ref, bias_ref, o_ref = refs[2 * CA_KEY_BLOCKS:]
    back = CA_BLOCKS - 1
    started = pl.program_id(2) > 0
    groups = [(qt, hd) for qt in range(CA_Q_TILES) for hd in range(PAIR)]
    scores, probs, outs = {}, {}, {}

    def score(qt, hd):
        q = _select_rows(q_ref[0, qt], hd * HEAD_DIM, (hd + 1) * HEAD_DIM)
        scores[qt, hd] = []
        for d in range(CA_BLOCKS):
            s = _dot(k_refs[qt + d][0], q) + bias_ref[hd, d]
            if qt + d < back:
                s = jnp.where(started, s, NEG_INF)
            scores[qt, hd].append(s)

    def normalise(qt, hd):
        ss = scores[qt, hd]
        m = functools.reduce(jnp.maximum, [jnp.max(s, axis=0, keepdims=True) for s in ss])
        probs[qt, hd] = [jnp.exp2(s - m).astype(BF16) for s in ss]

    def accumulate(qt, hd):
        acc = functools.reduce(jnp.add, [_dot(_with_ones_rows(v_refs[qt + d][0, 0, _head_rows(hd), :]),
                                              probs[qt, hd][d]) for d in range(CA_BLOCKS)])
        outs[qt, hd] = acc[:HEAD_DIM] / acc[HEAD_DIM:HEAD_DIM + 1]

    for stage in (score, normalise, accumulate):
        for qt, hd in groups:
            stage(qt, hd)
    for qt in range(CA_Q_TILES):
        out = jnp.concatenate([outs[qt, hd] for hd in range(PAIR)], axis=0)
        o_ref[0, qt] = (out * _silu(g_ref[0, qt].astype(F32))).astype(BF16)


def _ca_attention(qvg, k, bias):
    b, ns, _, t = qvg.shape
    back = CA_BLOCKS - 1
    qspec, gspec = _attn_specs(ns, k.shape[1], t, ROW_QB, COL_KB, ROW_VB, ROW_GB, CA_Q_TILES, False)

    def kspec(n):
        return pl.BlockSpec((1, t, LANES),
                            lambda bi, p, i: (bi, jnp.maximum(CA_Q_TILES * i - back + n, 0), COL_KB // LANES + p))

    def vspec(n):
        return pl.BlockSpec((1, 1, LANES, t),
                            lambda bi, p, i: (bi, jnp.maximum(CA_Q_TILES * i - back + n, 0), ROW_VB // LANES + p, 0))

    return pl.pallas_call(
        _ca_kernel,
        out_shape=jax.ShapeDtypeStruct((b, ns, CA_WIDTH, t), BF16),
        grid=(b, CA_HEADS // PAIR, ns // CA_Q_TILES),
        in_specs=[qspec] + [kspec(n) for n in range(CA_KEY_BLOCKS)] + [vspec(n) for n in range(CA_KEY_BLOCKS)]
        + [gspec, pl.BlockSpec((PAIR, CA_BLOCKS, t, t), lambda bi, p, i: (p, 0, 0, 0))],
        out_specs=pl.BlockSpec((1, CA_Q_TILES, LANES, t), lambda bi, p, i: (bi, i, p, 0)),
        compiler_params=pltpu.CompilerParams(
            dimension_semantics=("parallel", "parallel", "arbitrary"), vmem_limit_bytes=VMEM_LIMIT),
        name="ca_attn",
    )(qvg, *[k] * CA_KEY_BLOCKS, *[qvg] * CA_KEY_BLOCKS, qvg, bias)


def _ca_bias(rel_table):
    t = ATTN_TILE
    h = rel_table.shape[0]
    jk = np.arange(t)[:, None]
    iq = np.arange(t)[None, :]
    delta = np.arange(-(t - 1), t + 1)
    tiles = []
    for d in range(CA_BLOCKS):
        back = (CA_BLOCKS - 1 - d) * t
        idx = np.clip(back + delta, -MAX_REL, MAX_REL) + MAX_REL
        diag_vals = rel_table.astype(F32)[:, idx]
        flat = jnp.tile(diag_vals, (1, t))[:, :t * (2 * t - 1)]
        toeplitz = flat.reshape(h, t, 2 * t - 1)[:, :, t - 1:]
        lag = back // CHUNK + iq // CHUNK - jk // CHUNK
        valid = (lag >= 0) & (lag <= CA_LEFT_CHUNKS)
        tiles.append(jnp.where(valid[None], toeplitz * LOG2E, NEG_INF))
    return jnp.stack(tiles, axis=1)


def _da_kernel(q_ref, k_ref, v_ref, g_ref, lamv_ref, subg_ref, o_ref, *, lam_init):
    t = ATTN_TILE
    maps = range(2 * PAIR)
    qs = [[_select_rows(q_ref[0, qt], c * DA_QKDIM, (c + 1) * DA_QKDIM) for c in maps] for qt in range(DA_Q_TILES)]
    chunk_shift = int(math.log2(CHUNK))

    def tiles(state, keys, work):
        kts = [_key_tile(k_ref, j) for j in keys]
        mask = (lax.shift_right_logical(lax.broadcasted_iota(jnp.int32, (t, t), 0), chunk_shift)
                <= lax.shift_right_logical(lax.broadcasted_iota(jnp.int32, (t, t), 1), chunk_shift))
        groups = [(qt, c) for qt in range(DA_Q_TILES) for c in maps]
        scores, stats, probs, new = {}, {}, {}, {}

        def score(qt, c):
            scores[qt, c] = []
            for wq, kpos, diag in work:
                if wq == qt:
                    s = _dot(kts[kpos], qs[qt][c])
                    if diag:
                        s = jnp.where(mask, s, NEG_INF)
                    scores[qt, c].append((kpos, s))

        def normalise(qt, c):
            m = state[qt][c][0]
            m_new = functools.reduce(jnp.maximum,
                                     [m] + [jnp.max(s, axis=0, keepdims=True) for _, s in scores[qt, c]])
            stats[qt, c] = (m_new, jnp.exp2(m - m_new))
            probs[qt, c] = [(kpos, jnp.exp2(s - m_new).astype(BF16)) for kpos, s in scores[qt, c]]

        def accumulate(qt, c):
            m_new, alpha = stats[qt, c]
            acc = alpha * state[qt][c][1]
            for kpos, p in probs[qt, c]:
                acc = acc + _dot(_with_ones_rows(v_ref[0, keys[kpos], _head_rows(c // 2), :]), p)
            new[qt, c] = (m_new, acc)

        for stage in (score, normalise, accumulate):
            for qt, c in groups:
                stage(qt, c)
        return tuple(tuple(new[qt, c] for c in maps) for qt in range(DA_Q_TILES))

    init = tuple(tuple((jnp.full((1, t), NEG_INF, F32), jnp.zeros((HEAD_DIM + ONES_ROWS, t), F32))
                       for _ in maps) for _ in range(DA_Q_TILES))
    state = _causal_key_loop(pl.program_id(2), init, tiles, DA_Q_TILES, DA_KEYS_PER_TRIP)

    lamv = lamv_ref[...]
    lam = (jnp.exp(jnp.sum(lamv[0:1] * lamv[1:2], axis=-1, keepdims=True))
           - jnp.exp(jnp.sum(lamv[2:3] * lamv[3:4], axis=-1, keepdims=True)) + lam_init)
    for qt in range(DA_Q_TILES):
        outs = []
        for hd in range(PAIR):
            (_, acc1), (_, acc2) = state[qt][2 * hd], state[qt][2 * hd + 1]
            o = (acc1[:HEAD_DIM] / acc1[HEAD_DIM:HEAD_DIM + 1]
                 - lam * (acc2[:HEAD_DIM] / acc2[HEAD_DIM:HEAD_DIM + 1]))
            o = o * lax.rsqrt(jnp.mean(o * o, axis=0, keepdims=True) + EPS) * subg_ref[...]
            outs.append(o * (1.0 - lam_init))
        out = jnp.concatenate(outs, axis=0)
        o_ref[0, qt] = (out * _silu(g_ref[0, qt].astype(F32))).astype(BF16)


def _da_attention(qvg, k, lamv, subg, lam_init):
    b, ns, _, t = qvg.shape
    s = k.shape[1]
    return pl.pallas_call(
        functools.partial(_da_kernel, lam_init=lam_init),
        out_shape=jax.ShapeDtypeStruct((b, ns, DA_WIDTH, t), BF16),
        grid=(b, DA_HEADS // PAIR, ns // DA_Q_TILES),
        in_specs=[*_attn_specs(ns, s, t, ROW_QC, COL_KC, ROW_VC, ROW_GC, DA_Q_TILES, True),
                  pl.BlockSpec(lamv.shape, lambda bi, p, i: (0, 0)),
                  pl.BlockSpec(subg.shape, lambda bi, p, i: (0, 0))],
        out_specs=pl.BlockSpec((1, DA_Q_TILES, LANES, t), lambda bi, p, i: (bi, i, p, 0)),
        compiler_params=pltpu.CompilerParams(
            dimension_semantics=("parallel", "parallel", "arbitrary"), vmem_limit_bytes=VMEM_LIMIT),
        name="da_attn",
    )(qvg, k, qvg, qvg, lamv, subg)


def _out_proj_kernel(ya_ref, yb_ref, yc_ref, wo_ref, g_ref, x_ref, o_ref):
    t = ATTN_TILE
    for n in range(ya_ref.shape[1]):
        y_in = jnp.concatenate([ya_ref[0, n], yb_ref[0, n], yc_ref[0, n]], axis=0)
        y = lax.dot_general(y_in, wo_ref[...], (((0,), (0,)), ((), ())), preferred_element_type=F32)
        y = y * lax.rsqrt(jnp.mean(y * y, axis=-1, keepdims=True) + EPS)
        rows = slice(n * t, (n + 1) * t)
        o_ref[0, rows, :] = x_ref[0, rows, :] + y * g_ref[...]


def _out_proj(ya, yb, yc, wot, post_g, x):
    b, s, d = x.shape
    ts = min(PROJ_TILE, s)
    nt = ts // ATTN_TILE
    const = lambda *_: (0, 0)
    yspec = lambda w: pl.BlockSpec((1, nt, w, ATTN_TILE), lambda bi, si: (bi, si, 0, 0))
    return pl.pallas_call(
        _out_proj_kernel,
        out_shape=jax.ShapeDtypeStruct((b, s, d), F32),
        grid=(b, s // ts),
        in_specs=[yspec(SB_WIDTH), yspec(CA_WIDTH), yspec(DA_WIDTH),
                  pl.BlockSpec(wot.shape, const),
                  pl.BlockSpec((1, d), const),
                  pl.BlockSpec((1, ts, d), lambda bi, si: (bi, si, 0))],
        out_specs=pl.BlockSpec((1, ts, d), lambda bi, si: (bi, si, 0)),
        compiler_params=pltpu.CompilerParams(
            dimension_semantics=("parallel", "parallel"), vmem_limit_bytes=VMEM_LIMIT),
        name="out_proj",
    )(ya, yb, yc, wot, post_g, x)


def _split_w_in(w_in):
    a, bb, c = SB_WIDTH, CA_WIDTH, DA_WIDTH
    o_b, o_c = 4 * a, 4 * (a + bb)
    col = lambda o, w, n: w_in[:, o + n * w:o + (n + 1) * w]
    qvg = jnp.concatenate([col(0, a, 0), col(0, a, 2), col(0, a, 3),
                           col(o_b, bb, 0), col(o_b, bb, 2), col(o_b, bb, 3),
                           col(o_c, c, 0), col(o_c, c, 2), col(o_c, c, 3)], axis=1)
    wt = qvg.T.astype(BF16)
    wk = jnp.concatenate([col(0, a, 1), col(o_b, bb, 1)], axis=1).astype(BF16)
    wkct = col(o_c, c, 1).T.astype(BF16)
    return wt, wk, wkct


def _layer(x, pos3, invf, neg_tri, w_in, w_out, pre_g, post_g, rel_table, lamv, subln_g, lam_init):
    wt, wk, wkct = _split_w_in(w_in)
    qvg, k = _in_proj(x, pre_g[None, :], wt, wk, wkct, pos3, invf)
    ya = _sb_attention(qvg, k, neg_tri)
    yb = _ca_attention(qvg, k, _ca_bias(rel_table))
    yc = _da_attention(qvg, k, lamv, subln_g[:, None], lam_init)
    return _out_proj(ya, yb, yc, w_out.astype(BF16), post_g[None, :], x)


def kernel(x, positions, w_in, w_out, pre_gain, post_gain, rel_bias, lambda_q1, lambda_k1, lambda_q2,
           lambda_k2, subln_gain):
    b, s, d = x.shape
    assert d == D_MODEL and s % PROJ_TILE == 0
    half = DA_QKDIM // 2
    invf = jnp.asarray((ROPE_THETA ** (-np.arange(half, dtype=np.float32) / half))[:, None])
    pos3 = positions.reshape(b, 1, s)
    t = ATTN_TILE
    neg_tri = jnp.asarray(-(np.arange(t)[None, :] >= np.arange(t)[:, None]).astype(np.float32), dtype=BF16)
    for layer in range(w_in.shape[0]):
        lam_init = 0.8 - 0.6 * math.exp(-0.3 * layer)
        lamv = jnp.stack([lambda_q1[layer], lambda_k1[layer], lambda_q2[layer], lambda_k2[layer]])
        x = _layer(x, pos3, invf, neg_tri, w_in[layer], w_out[layer], pre_gain[layer], post_gain[layer],
                   rel_bias[layer], lamv, subln_gain[layer], lam_init)
    return x
```
